```python
import math
import jax
import jax.numpy as jnp
from jax import lax
import numpy as np

D_MODEL = 2048
BATCH = 2
SEQ = 4096
DEPTH = 4
DEC_BATCH = 8
DEC_SEQ = 1
PAST_LEN = 16384
PAGE_SIZE = 128

SSM_WIDTH = D_MODEL // 4
SSM_GROUP = 16
SSM_GROUPS = SSM_WIDTH // SSM_GROUP
SSM_STATE = 64
SSM_DT_MIN = 1e-3
SSM_DT_MAX = 1e-1
MLSTM_WIDTH = D_MODEL // 4
MLSTM_HEADS = 4
MLSTM_HEAD_DIM = MLSTM_WIDTH // MLSTM_HEADS
MLSTM_CHUNK = 64
FOX_WIDTH = D_MODEL // 2
FOX_HEADS = 8
FOX_HEAD_DIM = FOX_WIDTH // FOX_HEADS
FOX_QBLOCK = 128
FOX_F_BIAS = 3.0

N_BRANCH = 3
LN_EPS = 1e-5
DN_ALPHA = (2 * DEPTH) ** 0.25
DN_BETA = (8 * DEPTH) ** -0.25

IN_SPLITS = (
    ('ssm_u', SSM_WIDTH), ('ssm_z', SSM_WIDTH),
    ('m_q', MLSTM_WIDTH), ('m_k', MLSTM_WIDTH), ('m_v', MLSTM_WIDTH),
    ('m_i', MLSTM_HEADS), ('m_f', MLSTM_HEADS), ('m_o', MLSTM_WIDTH), ('m_z', MLSTM_WIDTH),
    ('c_q', FOX_WIDTH), ('c_k', FOX_WIDTH), ('c_v', FOX_WIDTH), ('c_f', FOX_HEADS), ('c_z', FOX_WIDTH),
    ('gate', N_BRANCH * D_MODEL),
)
IN_NAMES = tuple(nm for nm, _ in IN_SPLITS)
IN_WIDTHS = tuple(wd for _, wd in IN_SPLITS)
N_IN = sum(IN_WIDTHS)

kernel_name = 'hybrid_s5_mlstm_fox_decoder_step'


def _in_range(name):
    i = IN_NAMES.index(name)
    start = sum(IN_WIDTHS[:i])
    return start, start + IN_WIDTHS[i]


def _split_in(h):
    cuts = [int(c) for c in np.cumsum(IN_WIDTHS)[:-1]]
    return dict(zip(IN_NAMES, jnp.split(h, cuts, axis=-1)))


def _layer_norm(x, g, b):
    xf = x.astype(jnp.float32)
    mu = jnp.mean(xf, axis=-1, keepdims=True)
    var = jnp.mean(jnp.square(xf - mu), axis=-1, keepdims=True)
    return (xf - mu) * lax.rsqrt(var + LN_EPS) * g.astype(jnp.float32) + b.astype(jnp.float32)


def _head_norm(h):
    mu = jnp.mean(h, axis=-1, keepdims=True)
    var = jnp.mean(jnp.square(h - mu), axis=-1, keepdims=True)
    return (h - mu) * lax.rsqrt(var + LN_EPS)


def _cmul(ar, ai, br, bi):
    return ar * br - ai * bi, ar * bi + ai * br


def _s5_scan(u, x0_re, x0_im, a_re, a_im, b_re, b_im, c_re, c_im, d, log_dt):
    f32 = jnp.float32
    n, t, _ = u.shape
    uf = u.astype(f32).reshape(n, t, SSM_GROUPS, SSM_GROUP)
    a_re, a_im = a_re.astype(f32), a_im.astype(f32)
    dt = jnp.exp(log_dt.astype(f32))[:, None]
    mag = jnp.exp(a_re * dt)
    ab_re, ab_im = mag * jnp.cos(a_im * dt), mag * jnp.sin(a_im * dt)
    nr, ni = ab_re - 1.0, ab_im
    den = a_re * a_re + a_im * a_im
    k_re, k_im = (nr * a_re + ni * a_im) / den, (ni * a_re - nr * a_im) / den
    bu_re = jnp.einsum('ntgc,gpc->ntgp', uf, b_re.astype(f32))
    bu_im = jnp.einsum('ntgc,gpc->ntgp', uf, b_im.astype(f32))
    bu_re, bu_im = _cmul(k_re, k_im, bu_re, bu_im)
    cr, ci = _cmul(ab_re, ab_im, x0_re.astype(f32), x0_im.astype(f32))
    bu_re = bu_re.at[:, 0].add(cr)
    bu_im = bu_im.at[:, 0].add(ci)
    a_seq_re = jnp.broadcast_to(ab_re, bu_re.shape)
    a_seq_im = jnp.broadcast_to(ab_im, bu_im.shape)

    def combine(e1, e2):
        a1r, a1i, b1r, b1i = e1
        a2r, a2i, b2r, b2i = e2
        ar, ai = _cmul(a2r, a2i, a1r, a1i)
        br, bi = _cmul(a2r, a2i, b1r, b1i)
        return ar, ai, br + b2r, bi + b2i

    _, _, xr, xi = lax.associative_scan(combine, (a_seq_re, a_seq_im, bu_re, bu_im), axis=1)
    y = (jnp.einsum('gcp,ntgp->ntgc', c_re.astype(f32), xr)
         - jnp.einsum('gcp,ntgp->ntgc', c_im.astype(f32), xi)
         + d.astype(f32).reshape(SSM_GROUPS, SSM_GROUP) * uf)
    return y.reshape(n, t, SSM_WIDTH), xr[:, -1], xi[:, -1]


def _mlstm_chunk(carry, inp):
    c0, n0, m0 = carry
    q, k, v, li, lf = inp
    length = q.shape[2]
    b = jnp.cumsum(lf, axis=-1)
    causal = jnp.tril(jnp.ones((length, length), dtype=bool))
    dmat = jnp.where(causal, b[..., :, None] - b[..., None, :] + li[..., None, :], -jnp.inf)
    inter = b + m0[..., None]
    m = jnp.maximum(inter, jnp.max(dmat, axis=-1))
    w = jnp.exp(dmat - m[..., None])
    s_inter = jnp.exp(inter - m)
    qk = jnp.einsum('nhtd,nhsd->nhts', q, k) * w
    num = s_inter[..., None] * jnp.einsum('nhtd,nhde->nhte', q, c0) + jnp.einsum('nhts,nhse->nhte', qk, v)
    den = s_inter * jnp.einsum('nhtd,nhd->nht', q, n0) + jnp.sum(qk, axis=-1)
    h = num / jnp.maximum(jnp.abs(den), jnp.exp(-m))[..., None]
    m_end = m[..., -1]
    decay = jnp.exp(b[..., -1] + m0 - m_end)
    ws = jnp.exp(b[..., -1:] - b + li - m_end[..., None])
    c1 = decay[..., None, None] * c0 + jnp.einsum('nhs,nhsd,nhse->nhde', ws, k, v)
    n1 = decay[..., None] * n0 + jnp.einsum('nhs,nhsd->nhd', ws, k)
    return (c1, n1, m_end), h


def _mlstm(q, k, v, li, lf, c0, n0, m0):
    f32 = jnp.float32
    n, t, h, d = q.shape
    length = math.gcd(t, MLSTM_CHUNK)
    nc = t // length

    def chunks(a):
        a = a.reshape((n, nc, length) + a.shape[2:])
        return jnp.transpose(a, (1, 0, 3, 2) + tuple(range(4, a.ndim)))

    carry0 = (c0.astype(f32), n0.astype(f32), m0.astype(f32))
    carry, hs = lax.scan(_mlstm_chunk, carry0, (chunks(q), chunks(k), chunks(v), chunks(li), chunks(lf)))
    hs = jnp.transpose(hs, (1, 0, 3, 2, 4)).reshape(n, t, h, d)
    return hs, carry


def _fox_prompt(q, k, v, lf):
    f32 = jnp.float32
    n, t, h, dh = q.shape
    qb = min(FOX_QBLOCK, t)
    scale = dh ** -0.5
    f_cum = jnp.transpose(jnp.cumsum(lf, axis=1), (0, 2, 1))
    key_pos = jnp.arange(t)
    vf = v.astype(f32)

    def block(start):
        q_blk = lax.dynamic_slice_in_dim(q, start, qb, axis=1)
        f_blk = lax.dynamic_slice_in_dim(f_cum, start, qb, axis=2)
        s = (jnp.einsum('nqhd,nkhd->nhqk', q_blk, k, preferred_element_type=f32) * scale
             + f_blk[..., :, None] - f_cum[..., None, :])
        mask = (start + jnp.arange(qb))[:, None] >= key_pos[None, :]
        p = jax.nn.softmax(jnp.where(mask, s, -jnp.inf), axis=-1)
        return jnp.einsum('nhqk,nkhd->nqhd', p, vf)

    starts = jnp.arange(t // qb) * qb
    o = lax.map(block, starts)
    return jnp.transpose(o, (1, 0, 2, 3, 4)).reshape(n, t, h, dh)


def _fox_sample(q, k, v, lf, cache_k, cache_v, cache_logf, page_table, layer):
    f32 = jnp.float32
    n, s, h, dh = q.shape
    past = page_table.shape[1] * cache_k.shape[2]
    k_all = jnp.concatenate([cache_k[layer, page_table].reshape(n, past, h, dh), k], axis=1)
    v_all = jnp.concatenate([cache_v[layer, page_table].reshape(n, past, h, dh), v], axis=1)
    lf_all = jnp.concatenate([cache_logf[layer, page_table].reshape(n, past, h).astype(f32), lf], axis=1)
    f_cum = jnp.transpose(jnp.cumsum(lf_all, axis=1), (0, 2, 1))
    f_q = f_cum[..., past:]
    sc = (jnp.einsum('nqhd,nkhd->nhqk', q, k_all, preferred_element_type=f32) * dh ** -0.5
          + f_q[..., :, None] - f_cum[..., None, :])
    mask = jnp.arange(past + s)[None, :] <= past + jnp.arange(s)[:, None]
    p = jax.nn.softmax(jnp.where(mask, sc, -jnp.inf), axis=-1)
    return jnp.einsum('nhqk,nkhd->nqhd', p, v_all.astype(f32))


def _layer(x, lp, ssm_re, ssm_im, m_c, m_n, m_m, attend):
    f32 = jnp.float32
    n, t, _ = x.shape
    p = _split_in(jnp.einsum('ntd,de->nte', x, lp['w_in']) + lp['b_in'])
    y_s, ssm_re1, ssm_im1 = _s5_scan(p['ssm_u'], ssm_re, ssm_im, lp['ssm_a_re'], lp['ssm_a_im'],
                                     lp['ssm_b_re'], lp['ssm_b_im'], lp['ssm_c_re'], lp['ssm_c_im'],
                                     lp['ssm_d'], lp['ssm_log_dt'])
    g = jax.nn.gelu(y_s)
    y_a = (g * jax.nn.sigmoid(g @ lp['w_glu'].astype(f32) + lp['b_glu'].astype(f32))
           * jax.nn.silu(p['ssm_z'].astype(f32)))
    def mh(a):
        return a.astype(f32).reshape(n, t, MLSTM_HEADS, MLSTM_HEAD_DIM)
    h_m, (c1, n1, m1) = _mlstm(mh(p['m_q']), mh(p['m_k']) * MLSTM_HEAD_DIM ** -0.5, mh(p['m_v']),
                               p['m_i'].astype(f32), jax.nn.log_sigmoid(p['m_f'].astype(f32)),
                               m_c, m_n, m_m)
    h_m = _head_norm(h_m) * lp['mlstm_norm_g'].astype(f32).reshape(MLSTM_HEADS, MLSTM_HEAD_DIM)
    y_b = (h_m.reshape(n, t, MLSTM_WIDTH) * jax.nn.sigmoid(p['m_o'].astype(f32))
           * jax.nn.silu(p['m_z'].astype(f32)))
    def fh(a):
        return a.reshape(n, t, FOX_HEADS, FOX_HEAD_DIM)
    k_c, v_c = fh(p['c_k']), fh(p['c_v'])
    lf_c = jax.nn.log_sigmoid(p['c_f'].astype(f32))
    o_c = attend(fh(p['c_q']), k_c, v_c, lf_c)
    y_c = o_c.reshape(n, t, FOX_WIDTH) * jax.nn.silu(p['c_z'].astype(f32))
    g_a, g_b, g_c = jnp.split(jax.nn.sigmoid(p['gate'].astype(f32)), N_BRANCH, axis=-1)
    merged = g_a * (y_a @ lp['w_pa']) + g_b * (y_b @ lp['w_pb']) + g_c * (y_c @ lp['w_pc'])
    out = merged @ lp['w_out']
    x_new = _layer_norm(DN_ALPHA * x.astype(f32) + out, lp['ln_g'], lp['ln_b'])
    return x_new, (k_c, v_c, lf_c), (ssm_re1, ssm_im1), (c1, n1, m1)


def setup_inputs(seed: int = 0) -> dict:
    key = jax.random.key(seed)
    kit = iter(jax.random.split(key, 40))
    f32 = jnp.float32

    def nrm(shape, s=1.0):
        return s * jax.random.normal(next(kit), shape, f32)

    n_pages = PAST_LEN // PAGE_SIZE
    n_pool = (5 * DEC_BATCH * n_pages) // 4
    x_prompt = nrm((BATCH, SEQ, D_MODEL))
    x_sample = nrm((DEC_BATCH, DEC_SEQ, D_MODEL))
    cache_k = nrm((DEPTH, n_pool, PAGE_SIZE, FOX_HEADS, FOX_HEAD_DIM))
    cache_v = nrm((DEPTH, n_pool, PAGE_SIZE, FOX_HEADS, FOX_HEAD_DIM))
    cache_logf = jax.nn.log_sigmoid(FOX_F_BIAS + nrm((DEPTH, n_pool, PAGE_SIZE, FOX_HEADS)))
    page_table = jax.random.permutation(next(kit), n_pool)[:DEC_BATCH * n_pages].reshape(
        DEC_BATCH, n_pages).astype(jnp.int32)
    state_ssm_re = nrm((DEPTH, DEC_BATCH, SSM_GROUPS, SSM_STATE), 0.5)
    state_ssm_im = nrm((DEPTH, DEC_BATCH, SSM_GROUPS, SSM_STATE), 0.5)
    state_mlstm_c = nrm((DEPTH, DEC_BATCH, MLSTM_HEADS, MLSTM_HEAD_DIM, MLSTM_HEAD_DIM), 0.1)
    state_mlstm_n = nrm((DEPTH, DEC_BATCH, MLSTM_HEADS, MLSTM_HEAD_DIM), 0.1)
    state_mlstm_m = nrm((DEPTH, DEC_BATCH, MLSTM_HEADS))

    w_in = nrm((DEPTH, D_MODEL, N_IN), D_MODEL ** -0.5)
    b_in = nrm((DEPTH, N_IN), 0.02)
    fs, fe = _in_range('m_f')
    b_in = b_in.at[:, fs:fe].add(jnp.linspace(3.0, 6.0, MLSTM_HEADS))
    cs, ce = _in_range('c_f')
    b_in = b_in.at[:, cs:ce].add(FOX_F_BIAS)
    ssm_a_re = -0.5 + nrm((DEPTH, SSM_GROUPS, SSM_STATE), 0.01)
    ssm_a_im = math.pi * jnp.arange(SSM_STATE, dtype=f32) + nrm((DEPTH, SSM_GROUPS, SSM_STATE), 0.01)
    ssm_b_re = nrm((DEPTH, SSM_GROUPS, SSM_STATE, SSM_GROUP), (2 * SSM_GROUP) ** -0.5)
    ssm_b_im = nrm((DEPTH, SSM_GROUPS, SSM_STATE, SSM_GROUP), (2 * SSM_GROUP) ** -0.5)
    ssm_c_re = nrm((DEPTH, SSM_GROUPS, SSM_GROUP, SSM_STATE), (2 * SSM_STATE) ** -0.5)
    ssm_c_im = nrm((DEPTH, SSM_GROUPS, SSM_GROUP, SSM_STATE), (2 * SSM_STATE) ** -0.5)
    ssm_d = nrm((DEPTH, SSM_WIDTH))
    ssm_log_dt = jax.random.uniform(next(kit), (DEPTH, SSM_GROUPS), f32,
                                    minval=math.log(SSM_DT_MIN), maxval=math.log(SSM_DT_MAX))
    w_glu = nrm((DEPTH, SSM_WIDTH, SSM_WIDTH), SSM_WIDTH ** -0.5)
    b_glu = nrm((DEPTH, SSM_WIDTH), 0.02)
    mlstm_norm_g = 1.0 + nrm((DEPTH, MLSTM_WIDTH), 0.02)
    w_pa = nrm((DEPTH, SSM_WIDTH, D_MODEL), SSM_WIDTH ** -0.5)
    w_pb = nrm((DEPTH, MLSTM_WIDTH, D_MODEL), MLSTM_WIDTH ** -0.5)
    w_pc = nrm((DEPTH, FOX_WIDTH, D_MODEL), FOX_WIDTH ** -0.5)
    w_out = nrm((DEPTH, D_MODEL, D_MODEL), D_MODEL ** -0.5 * DN_BETA)
    ln_g = 1.0 + nrm((DEPTH, D_MODEL), 0.02)
    ln_b = nrm((DEPTH, D_MODEL), 0.02)
    return {'x_prompt': x_prompt, 'x_sample': x_sample, 'cache_k': cache_k, 'cache_v': cache_v,
            'cache_logf': cache_logf, 'page_table': page_table, 'state_ssm_re': state_ssm_re,
            'state_ssm_im': state_ssm_im, 'state_mlstm_c': state_mlstm_c, 'state_mlstm_n': state_mlstm_n,
            'state_mlstm_m': state_mlstm_m, 'w_in': w_in, 'b_in': b_in, 'ssm_a_re': ssm_a_re,
            'ssm_a_im': ssm_a_im, 'ssm_b_re': ssm_b_re, 'ssm_b_im': ssm_b_im, 'ssm_c_re': ssm_c_re,
            'ssm_c_im': ssm_c_im, 'ssm_d': ssm_d, 'ssm_log_dt': ssm_log_dt, 'w_glu': w_glu, 'b_glu': b_glu,
            'mlstm_norm_g': mlstm_norm_g, 'w_pa': w_pa, 'w_pb': w_pb, 'w_pc': w_pc, 'w_out': w_out,
            'ln_g': ln_g, 'ln_b': ln_b}


def reference(x_prompt, x_sample, cache_k, cache_v, cache_logf, page_table, state_ssm_re, state_ssm_im,
              state_mlstm_c, state_mlstm_n, state_mlstm_m, w_in, b_in, ssm_a_re, ssm_a_im, ssm_b_re,
              ssm_b_im, ssm_c_re, ssm_c_im, ssm_d, ssm_log_dt, w_glu, b_glu, mlstm_norm_g, w_pa, w_pb,
              w_pc, w_out, ln_g, ln_b):
    f32 = jnp.float32
    nb = x_prompt.shape[0]
    zero_ssm = jnp.zeros((nb, SSM_GROUPS, SSM_STATE), f32)
    zero_c = jnp.zeros((nb, MLSTM_HEADS, MLSTM_HEAD_DIM, MLSTM_HEAD_DIM), f32)
    zero_n = jnp.zeros((nb, MLSTM_HEADS, MLSTM_HEAD_DIM), f32)
    zero_m = jnp.zeros((nb, MLSTM_HEADS), f32)
    xp, xs = x_prompt, x_sample
    kp_l, vp_l, lfp_l, ks_l, vs_l, lfs_l = [], [], [], [], [], []
    srp_l, sip_l, srs_l, sis_l = [], [], [], []
    mcp_l, mnp_l, mmp_l, mcs_l, mns_l, mms_l = [], [], [], [], [], []
    for l in range(DEPTH):
        lp = {'w_in': w_in[l], 'b_in': b_in[l], 'ssm_a_re': ssm_a_re[l], 'ssm_a_im': ssm_a_im[l],
              'ssm_b_re': ssm_b_re[l], 'ssm_b_im': ssm_b_im[l], 'ssm_c_re': ssm_c_re[l],
              'ssm_c_im': ssm_c_im[l], 'ssm_d': ssm_d[l], 'ssm_log_dt': ssm_log_dt[l], 'w_glu': w_glu[l],
              'b_glu': b_glu[l], 'mlstm_norm_g': mlstm_norm_g[l], 'w_pa': w_pa[l], 'w_pb': w_pb[l],
              'w_pc': w_pc[l], 'w_out': w_out[l], 'ln_g': ln_g[l], 'ln_b': ln_b[l]}
        xp, rows_p, ssm_p, ml_p = _layer(xp, lp, zero_ssm, zero_ssm, zero_c, zero_n, zero_m, _fox_prompt)

        def attend_sample(q, k, v, lf, layer=l):
            return _fox_sample(q, k, v, lf, cache_k, cache_v, cache_logf, page_table, layer)

        xs, rows_s, ssm_s, ml_s = _layer(xs, lp, state_ssm_re[l], state_ssm_im[l], state_mlstm_c[l],
                                         state_mlstm_n[l], state_mlstm_m[l], attend_sample)
        kp_l.append(rows_p[0]); vp_l.append(rows_p[1]); lfp_l.append(rows_p[2])
        ks_l.append(rows_s[0]); vs_l.append(rows_s[1]); lfs_l.append(rows_s[2])
        srp_l.append(ssm_p[0]); sip_l.append(ssm_p[1]); srs_l.append(ssm_s[0]); sis_l.append(ssm_s[1])
        mcp_l.append(ml_p[0]); mnp_l.append(ml_p[1]); mmp_l.append(ml_p[2])
        mcs_l.append(ml_s[0]); mns_l.append(ml_s[1]); mms_l.append(ml_s[2])
    return (xp, xs,
            jnp.stack(kp_l), jnp.stack(vp_l), jnp.stack(lfp_l),
            jnp.stack(ks_l), jnp.stack(vs_l), jnp.stack(lfs_l),
            jnp.stack(srp_l), jnp.stack(sip_l), jnp.stack(srs_l), jnp.stack(sis_l),
            jnp.stack(mcp_l), jnp.stack(mnp_l), jnp.stack(mmp_l),
            jnp.stack(mcs_l), jnp.stack(mns_l), jnp.stack(mms_l))
```

```python
import functools
import math

import jax
import jax.numpy as jnp
from jax import lax
from jax.experimental import pallas as pl
from jax.experimental.pallas import tpu as pltpu

F32 = jnp.float32
BF16 = jnp.bfloat16

V7X_LANES = 128
V7X_SUBLANES = 8
V7X_VMEM_LIMIT_BYTES = 56 * 1024 * 1024

LN_EPS = 1e-5
NEG_INF = float("-inf")


def _cparams(n_axes):
    return pltpu.CompilerParams(dimension_semantics=("arbitrary",) * n_axes,
                                vmem_limit_bytes=V7X_VMEM_LIMIT_BYTES)


def _sigmoid(x):
    return 1.0 / (1.0 + jnp.exp(-x))


def _silu(x):
    return x * _sigmoid(x)


def _log_sigmoid(x):
    return jnp.minimum(x, 0.0) - jnp.log1p(jnp.exp(-jnp.abs(x)))


def _gelu_tanh(x):
    c = math.sqrt(2.0 / math.pi)
    return 0.5 * x * (1.0 + jnp.tanh(c * (x + 0.044715 * (x * x * x))))


def _dot(a, b):
    return jnp.dot(a, b, preferred_element_type=F32)


def _dot_nt(a, b):
    return lax.dot_general(a, b, (((1,), (1,)), ((), ())), preferred_element_type=F32)


def _split3(x):
    hi = x.astype(BF16)
    r1 = x - hi.astype(F32)
    mid = r1.astype(BF16)
    lo = (r1 - mid.astype(F32)).astype(BF16)
    return hi, mid, lo


def _cumsum_rows(x):
    n = x.shape[0]
    r = lax.broadcasted_iota(jnp.int32, (n, n), 0)
    c = lax.broadcasted_iota(jnp.int32, (n, n), 1)
    tri = (r >= c).astype(BF16)
    hi, mid, lo = _split3(x)
    return _dot(tri, hi) + _dot(tri, mid) + _dot(tri, lo)


def _cumsum_lanes(x):
    n = x.shape[1]
    r = lax.broadcasted_iota(jnp.int32, (n, n), 0)
    c = lax.broadcasted_iota(jnp.int32, (n, n), 1)
    tri = (r <= c).astype(BF16)
    hi, mid, lo = _split3(x)
    return _dot(hi, tri) + _dot(mid, tri) + _dot(lo, tri)


class _Dims:
    def __init__(self, x_prompt, x_sample, cache_k, page_table, state_mlstm_c, w_in, ssm_a_re, ssm_b_re):
        self.batch, self.seq, self.d = x_prompt.shape
        self.dec_batch = x_sample.shape[0]
        assert x_sample.shape[1] == 1
        self.depth = w_in.shape[0]
        self.g, self.p = ssm_a_re.shape[1], ssm_a_re.shape[2]
        self.gs = ssm_b_re.shape[3]
        self.sw = self.g * self.gs
        self.gp = self.g * self.p
        self.mh, self.md = state_mlstm_c.shape[2], state_mlstm_c.shape[3]
        self.mw = self.mh * self.md
        self.page, self.fh, self.fd = cache_k.shape[2], cache_k.shape[3], cache_k.shape[4]
        self.fw = self.fh * self.fd
        self.n_pages = page_table.shape[1]
        self.n_pool = cache_k.shape[1]
        d, sw, mw, fw, mh, fh = self.d, self.sw, self.mw, self.fw, self.mh, self.fh
        widths = (('ssm_u', sw), ('ssm_z', sw), ('m_q', mw), ('m_k', mw), ('m_v', mw), ('m_i', mh),
                  ('m_f', mh), ('m_o', mw), ('m_z', mw), ('c_q', fw), ('c_k', fw), ('c_v', fw),
                  ('c_f', fh), ('c_z', fw), ('gate', 3 * d))
        self.src = {}
        off = 0
        for nm, wd in widths:
            self.src[nm] = (off, off + wd)
            off += wd
        assert off == w_in.shape[2]
        self.main_order = ('gate', 'ssm_u', 'ssm_z', 'm_q', 'm_k', 'm_v', 'm_o', 'm_z', 'c_q', 'c_z')
        self.off = {}
        off = 0
        for nm in self.main_order:
            self.off[nm] = off
            off += self.src[nm][1] - self.src[nm][0]
        self.n_main = off
        self.n_kv = 2 * fw + V7X_LANES
        self.lane_li, self.lane_lf, self.lane_cf = 0, mh, 2 * mh
        assert 2 * mh + fh <= V7X_LANES
        assert self.md % V7X_LANES == 0 and self.fd % V7X_LANES == 0 and sw % V7X_LANES == 0
        assert fh == V7X_SUBLANES and self.page % V7X_LANES == 0
        for nm in ('ssm_u', 'ssm_z', 'm_q', 'm_k', 'm_v', 'm_o', 'm_z'):
            assert self.off[nm] % sw == 0 and self.off[nm] % mw == 0
        assert self.off['c_q'] % self.fd == 0 and self.off['c_z'] % self.fd == 0
        self.m_rows = self.batch * self.seq
        self.tm_in = min(512, self.m_rows)
        self.tn_main = max(t for t in range(V7X_LANES, 3072 + 1, V7X_LANES) if self.n_main % t == 0)
        self.chunk = min(256, self.seq)
        self.tq = min(512, self.seq)
        self.tm_merge = min(256, self.m_rows)
        assert self.seq % self.chunk == 0 and self.seq % self.tq == 0
        assert self.m_rows % self.tm_in == 0 and self.m_rows % self.tm_merge == 0


def _mm_bias_kernel(x_ref, w_ref, b_ref, o_ref):
    acc = _dot(x_ref[...].astype(BF16), w_ref[...]) + b_ref[...]
    o_ref[...] = acc.astype(o_ref.dtype)


def _inproj_main(xb, w_all, b_all, layer, tm, tn, out_dtype):
    m, d = xb.shape
    n = w_all.shape[2]
    return pl.pallas_call(
        _mm_bias_kernel,
        out_shape=jax.ShapeDtypeStruct((m, n), out_dtype),
        grid=(n // tn, m // tm),
        in_specs=[pl.BlockSpec((tm, d), lambda j, i: (i, 0)),
                  pl.BlockSpec((None, d, tn), lambda j, i: (layer, 0, j)),
                  pl.BlockSpec((None, 1, tn), lambda j, i: (layer, 0, j))],
        out_specs=pl.BlockSpec((tm, tn), lambda j, i: (i, j)),
        compiler_params=_cparams(2),
        name="inproj_main",
    )(xb, w_all, b_all)


def _kv_kernel(x_ref, w_ref, b_ref, k32_ref, v32_ref, s32_ref, kb_ref, vb_ref, *, fw):
    acc = _dot(x_ref[...].astype(BF16), w_ref[...]) + b_ref[...]
    k = acc[:, :fw]
    v = acc[:, fw:2 * fw]
    k32_ref[...] = k
    v32_ref[...] = v
    s32_ref[...] = acc[:, 2 * fw:]
    kb_ref[...] = k.astype(BF16)
    vb_ref[...] = v.astype(BF16)


def _inproj_kv(xb, w_all, b_all, layer, tm, fw):
    m, d = xb.shape
    n = w_all.shape[2]
    row = lambda i: (i, 0)
    return pl.pallas_call(
        functools.partial(_kv_kernel, fw=fw),
        out_shape=(jax.ShapeDtypeStruct((m, fw), F32), jax.ShapeDtypeStruct((m, fw), F32),
                   jax.ShapeDtypeStruct((m, V7X_LANES), F32),
                   jax.ShapeDtypeStruct((m, fw), BF16), jax.ShapeDtypeStruct((m, fw), BF16)),
        grid=(m // tm,),
        in_specs=[pl.BlockSpec((tm, d), row),
                  pl.BlockSpec((None, d, n), lambda i: (layer, 0, 0)),
                  pl.BlockSpec((None, 1, n), lambda i: (layer, 0, 0))],
        out_specs=(pl.BlockSpec((tm, fw), row), pl.BlockSpec((tm, fw), row),
                   pl.BlockSpec((tm, V7X_LANES), row),
                   pl.BlockSpec((tm, fw), row), pl.BlockSpec((tm, fw), row)),
        compiler_params=_cparams(1),
        name="inproj_kv",
    )(xb, w_all, b_all)


def _gates_kernel(s_ref, lf_ref, bc_ref, fc_ref, st_ref, bct_ref, fct_ref, carry_ref):
    @pl.when(pl.program_id(1) == 0)
    def _():
        carry_ref[...] = jnp.zeros_like(carry_ref)

    s = s_ref[...]
    lf = _log_sigmoid(s)
    bc = _cumsum_rows(lf)
    fc = bc + carry_ref[0:1, :]
    carry_ref[...] = jnp.broadcast_to(fc[-1:, :], carry_ref.shape)
    lf_ref[...] = lf
    bc_ref[...] = bc
    fc_ref[...] = fc
    st_ref[...] = s.T
    bct_ref[...] = bc.T
    fct_ref[...] = fc.T


def _gates(s32, n_seq, seq, chunk):
    m = s32.shape[0]
    nc = seq // chunk
    row = pl.BlockSpec((chunk, V7X_LANES), lambda n, c: (n * nc + c, 0))
    tr = pl.BlockSpec((None, V7X_LANES, chunk), lambda n, c: (n, 0, c))
    rs = jax.ShapeDtypeStruct((m, V7X_LANES), F32)
    ts = jax.ShapeDtypeStruct((n_seq, V7X_LANES, seq), F32)
    return pl.pallas_call(
        _gates_kernel,
        out_shape=(rs, rs, rs, ts, ts, ts),
        grid=(n_seq, nc),
        in_specs=[row],
        out_specs=(row, row, row, tr, tr, tr),
        scratch_shapes=[pltpu.VMEM((V7X_SUBLANES, V7X_LANES), F32)],
        compiler_params=_cparams(2),
        name="gates",
    )(s32)


def _logsig_kernel(s_ref, o_ref):
    o_ref[...] = _log_sigmoid(s_ref[...])


def _logsig(s32):
    return pl.pallas_call(_logsig_kernel, out_shape=jax.ShapeDtypeStruct(s32.shape, F32), name="logsig")(s32)


def _s5_disc_kernel(are_ref, aim_ref, ldt_ref, o_ref):
    a_re, a_im = are_ref[...], aim_ref[...]
    dt = jnp.exp(ldt_ref[...])
    mag = jnp.exp(a_re * dt)
    ab_re, ab_im = mag * jnp.cos(a_im * dt), mag * jnp.sin(a_im * dt)
    nr, ni = ab_re - 1.0, ab_im
    den = a_re * a_re + a_im * a_im
    o_ref[0] = ab_re
    o_ref[1] = ab_im
    o_ref[2] = (nr * a_re + ni * a_im) / den
    o_ref[3] = (ni * a_re - nr * a_im) / den


def _s5_disc(a_re, a_im, log_dt):
    g, p = a_re.shape
    ldt = jnp.broadcast_to(log_dt[:, None], (g, p))
    return pl.pallas_call(_s5_disc_kernel, out_shape=jax.ShapeDtypeStruct((4, g, p), F32),
                          name="s5_disc")(a_re, a_im, ldt)


def _s5_tail(x_state, u, z, cmat, dvec, wglu, bglu):
    y = _dot(x_state.astype(BF16), cmat) + dvec * u
    gact = _gelu_tanh(y)
    glu = _dot(gact.astype(BF16), wglu) + bglu
    return gact * _sigmoid(glu) * _silu(z)


def _s5_prompt_kernel(u_ref, z_ref, bmat_ref, cmat_ref, disc_ref, d_ref, wglu_ref, bglu_ref,
                      ya_ref, st_ref, buf_ref, carry_ref, *, gp):
    c = pl.program_id(1)

    @pl.when(c == 0)
    def _():
        carry_ref[...] = jnp.zeros_like(carry_ref)

    u = u_ref[...]
    bu = _dot(u, bmat_ref[...])
    br, bi = bu[:, :gp], bu[:, gp:]
    ar, ai = disc_ref[0:1, :], disc_ref[1:2, :]
    kr, ki = disc_ref[2:3, :], disc_ref[3:4, :]
    buf_ref[:, :gp] = kr * br - ki * bi
    buf_ref[:, gp:] = kr * bi + ki * br

    def step(t, carry):
        xr, xi = carry
        nr = ar * xr - ai * xi + buf_ref[pl.ds(t, 1), :gp]
        ni = ar * xi + ai * xr + buf_ref[pl.ds(t, 1), gp:]
        buf_ref[pl.ds(t, 1), :gp] = nr
        buf_ref[pl.ds(t, 1), gp:] = ni
        return nr, ni

    xr, xi = lax.fori_loop(0, u.shape[0], step, (carry_ref[0:1, :], carry_ref[1:2, :]))
    carry_ref[0:1, :] = xr
    carry_ref[1:2, :] = xi
    ya = _s5_tail(buf_ref[...], u.astype(F32), z_ref[...].astype(F32), cmat_ref[...], d_ref[...],
                  wglu_ref[...], bglu_ref[...])
    ya_ref[...] = ya.astype(ya_ref.dtype)

    @pl.when(c == pl.num_programs(1) - 1)
    def _():
        st_ref[...] = carry_ref[...]


def _s5_prompt(pm, dm, layer, bmat, cmat, disc, dvec, wglu, bglu):
    tc = dm.chunk
    nc = dm.seq // tc
    sw, gp = dm.sw, dm.gp
    cu, cz = dm.off['ssm_u'] // sw, dm.off['ssm_z'] // sw
    const2 = lambda n, c: (0, 0)
    lay3 = lambda n, c: (layer, 0, 0)
    return pl.pallas_call(
        functools.partial(_s5_prompt_kernel, gp=gp),
        out_shape=(jax.ShapeDtypeStruct((dm.m_rows, sw), BF16),
                   jax.ShapeDtypeStruct((dm.batch, V7X_SUBLANES, gp), F32)),
        grid=(dm.batch, nc),
        in_specs=[pl.BlockSpec((tc, sw), lambda n, c: (n * nc + c, cu)),
                  pl.BlockSpec((tc, sw), lambda n, c: (n * nc + c, cz)),
                  pl.BlockSpec(bmat.shape, const2), pl.BlockSpec(cmat.shape, const2),
                  pl.BlockSpec(disc.shape, const2),
                  pl.BlockSpec((None, 1, sw), lay3), pl.BlockSpec((None, sw, sw), lay3),
                  pl.BlockSpec((None, 1, sw), lay3)],
        out_specs=(pl.BlockSpec((tc, sw), lambda n, c: (n * nc + c, 0)),
                   pl.BlockSpec((None, V7X_SUBLANES, gp), lambda n, c: (n, 0, 0))),
        scratch_shapes=[pltpu.VMEM((tc, 2 * gp), F32), pltpu.VMEM((V7X_SUBLANES, gp), F32)],
        compiler_params=_cparams(2),
        name="s5_prompt",
    )(pm, pm, bmat, cmat, disc, dvec, wglu, bglu)


def _s5_sample_kernel(u_ref, z_ref, x0r_ref, x0i_ref, bmat_ref, cmat_ref, disc_ref, d_ref, wglu_ref,
                      bglu_ref, ya_ref, x1r_ref, x1i_ref, *, gp):
    u = u_ref[...]
    bu = _dot(u.astype(BF16), bmat_ref[...])
    br, bi = bu[:, :gp], bu[:, gp:]
    ar, ai = disc_ref[0:1, :], disc_ref[1:2, :]
    kr, ki = disc_ref[2:3, :], disc_ref[3:4, :]
    x0r, x0i = x0r_ref[...], x0i_ref[...]
    x1r = (kr * br - ki * bi) + (ar * x0r - ai * x0i)
    x1i = (kr * bi + ki * br) + (ar * x0i + ai * x0r)
    x1r_ref[...] = x1r
    x1i_ref[...] = x1i
    ya = _s5_tail(jnp.concatenate([x1r, x1i], axis=1), u, z_ref[...], cmat_ref[...], d_ref[0],
                  wglu_ref[0], bglu_ref[0])
    ya_ref[...] = ya


def _s5_sample(ps, dm, layer, x0r, x0i, bmat, cmat, disc, dvec, wglu, bglu):
    n, sw, gp = dm.dec_batch, dm.sw, dm.gp
    cu, cz = dm.off['ssm_u'] // sw, dm.off['ssm_z'] // sw
    full = lambda a: pl.BlockSpec(a.shape, lambda i: (0,) * a.ndim)
    lay3 = lambda i: (layer, 0, 0)
    return pl.pallas_call(
        functools.partial(_s5_sample_kernel, gp=gp),
        out_shape=(jax.ShapeDtypeStruct((n, sw), F32), jax.ShapeDtypeStruct((n, gp), F32),
                   jax.ShapeDtypeStruct((n, gp), F32)),
        grid=(1,),
        in_specs=[pl.BlockSpec((n, sw), lambda i: (0, cu)), pl.BlockSpec((n, sw), lambda i: (0, cz)),
                  full(x0r), full(x0i), full(bmat), full(cmat), full(disc),
                  pl.BlockSpec((1, 1, sw), lay3), pl.BlockSpec((1, sw, sw), lay3),
                  pl.BlockSpec((1, 1, sw), lay3)],
        out_specs=(pl.BlockSpec((n, sw), lambda i: (0, 0)), pl.BlockSpec((n, gp), lambda i: (0, 0)),
                   pl.BlockSpec((n, gp), lambda i: (0, 0))),
        compiler_params=_cparams(1),
        name="s5_sample",
    )(ps, ps, x0r, x0i, bmat, cmat, disc, dvec, wglu, bglu)


def _head_norm(h):
    mu = jnp.mean(h, axis=-1, keepdims=True)
    var = jnp.mean(jnp.square(h - mu), axis=-1, keepdims=True)
    return (h - mu) * lax.rsqrt(var + LN_EPS)


def _mlstm_prompt_kernel(q_ref, k_ref, v_ref, o_ref, z_ref, bc_ref, st_ref, bct_ref, g_ref,
                         yb_ref, cout_ref, mout_ref, cst_ref, mst_ref, *, mh, md, lane_li, lane_lf):
    c = pl.program_id(1)

    @pl.when(c == 0)
    def _():
        cst_ref[...] = jnp.zeros_like(cst_ref)
        mst_ref[...] = jnp.zeros_like(mst_ref)

    ln = q_ref.shape[0]
    row = lax.broadcasted_iota(jnp.int32, (ln, ln), 0)
    col = lax.broadcasted_iota(jnp.int32, (ln, ln), 1)
    causal = row >= col
    ones_col = (lax.broadcasted_iota(jnp.int32, (ln, V7X_LANES), 1) == 0).astype(BF16)
    for h in range(mh):
        hs = slice(h * md, (h + 1) * md)
        q, k, v = q_ref[:, hs], k_ref[:, hs], v_ref[:, hs]
        a_row = st_ref[lane_li + h:lane_li + h + 1, :] - bct_ref[lane_lf + h:lane_lf + h + 1, :]
        b_col = bc_ref[:, lane_lf + h:lane_lf + h + 1]
        m0 = mst_ref[h:h + 1, 0:1]
        amat = jnp.where(causal, a_row, NEG_INF)
        m_col = jnp.maximum(m0, jnp.max(amat, axis=1, keepdims=True))
        w = jnp.exp(amat - m_col)
        s_inter = jnp.exp(m0 - m_col)
        qk = (_dot_nt(q, k) * w).astype(BF16)
        v_aug = jnp.concatenate([v, ones_col], axis=1)
        c0 = cst_ref[h]
        num_aug = s_inter * _dot(q, c0.astype(BF16)) + _dot(qk, v_aug)
        num, den = num_aug[:, :md], num_aug[:, md:md + 1]
        hh = num / jnp.maximum(jnp.abs(den), jnp.exp(-(b_col + m_col)))
        yb = (_head_norm(hh) * g_ref[:, hs] * _sigmoid(o_ref[:, hs].astype(F32))
              * _silu(z_ref[:, hs].astype(F32)))
        yb_ref[:, hs] = yb.astype(yb_ref.dtype)
        m_end = m_col[ln - 1:ln, :]
        decay = jnp.exp(m0 - m_end)
        ws_row = jnp.exp(a_row - m_end)
        kw = (k.astype(F32).T * ws_row).astype(BF16)
        cst_ref[h] = decay * c0 + _dot(kw, v_aug)
        mst_ref[h:h + 1, :] = jnp.broadcast_to(b_col[ln - 1:ln, :] + m_end, (1, V7X_LANES))

    @pl.when(c == pl.num_programs(1) - 1)
    def _():
        cout_ref[...] = cst_ref[...]
        mout_ref[...] = mst_ref[...]


def _mlstm_prompt(pm, bc, st, bct, norm_g, dm, layer):
    ln = dm.chunk
    nc = dm.seq // ln
    mw, mh, md = dm.mw, dm.mh, dm.md
    aug = md + V7X_LANES
    rowblk = lambda nm: pl.BlockSpec((ln, mw), lambda n, c, o=dm.off[nm] // mw: (n * nc + c, o))
    tr = pl.BlockSpec((None, V7X_LANES, ln), lambda n, c: (n, 0, c))
    return pl.pallas_call(
        functools.partial(_mlstm_prompt_kernel, mh=mh, md=md, lane_li=dm.lane_li, lane_lf=dm.lane_lf),
        out_shape=(jax.ShapeDtypeStruct((dm.m_rows, mw), BF16),
                   jax.ShapeDtypeStruct((dm.batch, mh, md, aug), F32),
                   jax.ShapeDtypeStruct((dm.batch, V7X_SUBLANES, V7X_LANES), F32)),
        grid=(dm.batch, nc),
        in_specs=[rowblk('m_q'), rowblk('m_k'), rowblk('m_v'), rowblk('m_o'), rowblk('m_z'),
                  pl.BlockSpec((ln, V7X_LANES), lambda n, c: (n * nc + c, 0)), tr, tr,
                  pl.BlockSpec((None, 1, mw), lambda n, c: (layer, 0, 0))],
        out_specs=(pl.BlockSpec((ln, mw), lambda n, c: (n * nc + c, 0)),
                   pl.BlockSpec((None, mh, md, aug), lambda n, c: (n, 0, 0, 0)),
                   pl.BlockSpec((None, V7X_SUBLANES, V7X_LANES), lambda n, c: (n, 0, 0))),
        scratch_shapes=[pltpu.VMEM((mh, md, aug), F32), pltpu.VMEM((V7X_SUBLANES, V7X_LANES), F32)],
        compiler_params=_cparams(2),
        name="mlstm_prompt",
    )(pm, pm, pm, pm, pm, bc, st, bct, norm_g)


def _mlstm_sample_kernel(ps_ref, s_ref, c0_ref, n0_ref, m0_ref, g_ref, yb_ref, c1_ref, n1_ref, m1_ref,
                         *, nseq, mh, md, offs, lane_li, lane_lf):
    s = s_ref[...]
    lfa = _log_sigmoid(s)
    eye = (lax.broadcasted_iota(jnp.int32, (md, md), 0) == lax.broadcasted_iota(jnp.int32, (md, md), 1))
    for n in range(nseq):
        for h in range(mh):
            def seg(nm):
                o = offs[nm] + h * md
                return ps_ref[n:n + 1, o:o + md]
            q = seg('m_q').astype(BF16)
            k = seg('m_k').astype(BF16)
            v = seg('m_v').astype(BF16).astype(F32)
            li = s[n:n + 1, lane_li + h:lane_li + h + 1]
            lf = lfa[n:n + 1, lane_lf + h:lane_lf + h + 1]
            m0 = m0_ref[n:n + 1, h:h + 1]
            c0 = c0_ref[n, h]
            n0 = n0_ref[n, h:h + 1, :]
            inter = lf + m0
            m1 = jnp.maximum(inter, li)
            w = jnp.exp(li - m1)
            s_inter = jnp.exp(inter - m1)
            kf = k.astype(F32)
            qk = jnp.sum(q.astype(F32) * kf, axis=1, keepdims=True) * w
            qc = _dot(jnp.broadcast_to(q, (V7X_SUBLANES, md)), c0.astype(BF16))[0:1, :]
            num = s_inter * qc + qk * v
            den = s_inter * jnp.sum(q.astype(F32) * n0, axis=1, keepdims=True) + qk
            hh = num / jnp.maximum(jnp.abs(den), jnp.exp(-m1))
            go = g_ref[0, :, h * md:(h + 1) * md]
            yb = _head_norm(hh) * go * _sigmoid(seg('m_o')) * _silu(seg('m_z'))
            yb_ref[n:n + 1, h * md:(h + 1) * md] = yb
            k_col = jnp.sum(jnp.where(eye, kf, 0.0), axis=1, keepdims=True)
            c1_ref[n, h] = s_inter * c0 + (w * k_col) * v
            n1_ref[n, h:h + 1, :] = s_inter * n0 + w * kf
            m1_ref[n:n + 1, h:h + 1] = m1


def _mlstm_sample(ps, s32, c0, n0, m0, norm_g, dm, layer):
    n, mh, md, mw = dm.dec_batch, dm.mh, dm.md, dm.mw
    offs = {nm: dm.off[nm] for nm in ('m_q', 'm_k', 'm_v', 'm_o', 'm_z')}
    full = lambda a: pl.BlockSpec(a.shape, lambda i: (0,) * a.ndim)
    st4 = lambda i: (layer, 0, 0, 0, 0)
    return pl.pallas_call(
        functools.partial(_mlstm_sample_kernel, nseq=n, mh=mh, md=md, offs=offs, lane_li=dm.lane_li,
                          lane_lf=dm.lane_lf),
        out_shape=(jax.ShapeDtypeStruct((n, mw), F32), jax.ShapeDtypeStruct((n, mh, md, md), F32),
                   jax.ShapeDtypeStruct((n, mh, md), F32), jax.ShapeDtypeStruct((n, mh), F32)),
        grid=(1,),
        in_specs=[full(ps), full(s32),
                  pl.BlockSpec((None, n, mh, md, md), st4),
                  pl.BlockSpec((None, n, mh, md), lambda i: (layer, 0, 0, 0)),
                  pl.BlockSpec((None, n, mh), lambda i: (layer, 0, 0)),
                  pl.BlockSpec((1, 1, mw), lambda i: (layer, 0, 0))],
        out_specs=(pl.BlockSpec((n, mw), lambda i: (0, 0)),
                   pl.BlockSpec((n, mh, md, md), lambda i: (0, 0, 0, 0)),
                   pl.BlockSpec((n, mh, md), lambda i: (0, 0, 0)),
                   pl.BlockSpec((n, mh), lambda i: (0, 0))),
        compiler_params=_cparams(1),
        name="mlstm_sample",
    )(ps, s32, c0, n0, m0, norm_g)


def _flash_kernel(q_ref, z_ref, k_ref, v_ref, fc_ref, fct_ref, yc_ref, *, tq, lane_cf):
    h = pl.program_id(1)
    qb = pl.program_id(2)
    q = q_ref[...]
    lane = lax.broadcasted_iota(jnp.int32, (tq, V7X_LANES), 1)
    fq = jnp.sum(jnp.where(lane == lane_cf + h, fc_ref[...], 0.0), axis=1, keepdims=True)
    row = lax.broadcasted_iota(jnp.int32, (tq, tq), 0)
    col = lax.broadcasted_iota(jnp.int32, (tq, tq), 1)
    fd = q.shape[1]

    def body(kb, carry):
        m_i, l_i, acc = carry
        ks = pl.multiple_of(kb * tq, tq)
        k = k_ref[pl.ds(ks, tq), :]
        v = v_ref[pl.ds(ks, tq), :]
        fk = fct_ref[pl.ds(lane_cf + h, 1), pl.ds(ks, tq)]
        s = _dot_nt(q, k) + (fq - fk)
        s = jnp.where((row + qb * tq) >= (col + kb * tq), s, NEG_INF)
        m_new = jnp.maximum(m_i, jnp.max(s, axis=1, keepdims=True))
        alpha = jnp.exp(m_i - m_new)
        p = jnp.exp(s - m_new)
        l_new = alpha * l_i + jnp.sum(p, axis=1, keepdims=True)
        acc_new = alpha * acc + _dot(p.astype(BF16), v)
        return m_new, l_new, acc_new

    init = (jnp.full((tq, 1), NEG_INF, F32), jnp.zeros((tq, 1), F32), jnp.zeros((tq, fd), F32))
    _, l_f, acc = lax.fori_loop(0, qb + 1, body, init)
    yc = (acc / l_f) * _silu(z_ref[...].astype(F32))
    yc_ref[...] = yc.astype(yc_ref.dtype)


def _flash(pm, kb, vb, fc, fct, dm):
    tq, fd, fh, seq = dm.tq, dm.fd, dm.fh, dm.seq
    nq = seq // tq
    cq, cz = dm.off['c_q'] // fd, dm.off['c_z'] // fd
    return pl.pallas_call(
        functools.partial(_flash_kernel, tq=tq, lane_cf=dm.lane_cf),
        out_shape=jax.ShapeDtypeStruct((dm.m_rows, dm.fw), BF16),
        grid=(dm.batch, fh, nq),
        in_specs=[pl.BlockSpec((tq, fd), lambda n, h, i: (n * nq + i, cq + h)),
                  pl.BlockSpec((tq, fd), lambda n, h, i: (n * nq + i, cz + h)),
                  pl.BlockSpec((seq, fd), lambda n, h, i: (n, h)),
                  pl.BlockSpec((seq, fd), lambda n, h, i: (n, h)),
                  pl.BlockSpec((tq, V7X_LANES), lambda n, h, i: (n * nq + i, 0)),
                  pl.BlockSpec((None, V7X_LANES, seq), lambda n, h, i: (n, 0, 0))],
        out_specs=pl.BlockSpec((tq, fd), lambda n, h, i: (n * nq + i, h)),
        compiler_params=_cparams(3),
        name="fox_prompt",
    )(pm, pm, kb, vb, fc, fct)


def _decode_kernel(pt_ref, qblk_ref, k_ref, v_ref, lft_ref, knew_ref, vnew_ref, lfnew_ref, z_ref, o_ref,
                   m_ref, l_ref, acc_ref, cs_ref, *, fh, fd):
    n = pl.program_id(0)
    j = pl.program_id(1)
    page = k_ref.shape[0]

    @pl.when(j == 0)
    def _():
        m_ref[...] = jnp.full_like(m_ref, NEG_INF)
        l_ref[...] = jnp.zeros_like(l_ref)
        acc_ref[...] = jnp.zeros_like(acc_ref)
        cs_ref[...] = jnp.zeros_like(cs_ref)

    cum = _cumsum_lanes(lft_ref[...]) + cs_ref[:, 0:1]
    cs_ref[...] = jnp.broadcast_to(cum[:, page - 1:page], cs_ref.shape)
    cum_sq = jnp.concatenate([cum, jnp.zeros((page - fh, page), F32)], axis=0)
    fcum = cum_sq.T
    qblk = qblk_ref[...]
    s = _dot(k_ref[...].astype(BF16), qblk) - fcum
    m_old = m_ref[0:1, :]
    m_new = jnp.maximum(m_old, jnp.max(s, axis=0, keepdims=True))
    alpha = jnp.exp(m_old - m_new)
    p = jnp.exp(s - m_new)
    l_new = alpha * l_ref[0:1, :] + jnp.sum(p, axis=0, keepdims=True)
    m_ref[...] = jnp.broadcast_to(m_new, m_ref.shape)
    l_ref[...] = jnp.broadcast_to(l_new, l_ref.shape)
    for h in range(fh):
        hs = slice(h * fd, (h + 1) * fd)
        contrib = (p[:, h:h + 1] * v_ref[:, hs]).reshape(page // V7X_SUBLANES, V7X_SUBLANES, fd)
        acc_ref[:, hs] = acc_ref[:, hs] * alpha[:, h:h + 1] + jnp.sum(contrib, axis=0)

    @pl.when(j == pl.num_programs(1) - 1)
    def _():
        knew = jnp.broadcast_to(knew_ref[pl.ds(n, 1), :], (V7X_SUBLANES, fh * fd))
        s_new = _dot(knew.astype(BF16), qblk)[0:1, :] - (fcum[page - 1:page, :] + lfnew_ref[pl.ds(n, 1), :])
        m_fin = jnp.maximum(m_new, s_new)
        a_fin = jnp.exp(m_new - m_fin)
        p_new = jnp.exp(s_new - m_fin)
        l_fin = a_fin * l_new + p_new
        vnew = vnew_ref[pl.ds(n, 1), :]
        outs = []
        for h in range(fh):
            hs = slice(h * fd, (h + 1) * fd)
            tot = jnp.sum(acc_ref[:, hs], axis=0, keepdims=True) * a_fin[:, h:h + 1]
            outs.append((tot + p_new[:, h:h + 1] * vnew[:, hs]) / l_fin[:, h:h + 1])
        o_ref[pl.ds(n, 1), :] = jnp.concatenate(outs, axis=1) * _silu(z_ref[pl.ds(n, 1), :])


def _decode(page_table, qblk, cache_k, cache_v, cache_lft, knew, vnew, lfnew, zs, dm, layer):
    n, fh, fd, fw, page = dm.dec_batch, dm.fh, dm.fd, dm.fw, dm.page
    full = lambda a: pl.BlockSpec(a.shape, lambda i, j, pt: (0,) * a.ndim)
    pg = lambda i, j, pt: (layer, pt[i, j], 0, 0)
    grid_spec = pltpu.PrefetchScalarGridSpec(
        num_scalar_prefetch=1,
        grid=(n, dm.n_pages),
        in_specs=[pl.BlockSpec((None, fw, V7X_LANES), lambda i, j, pt: (i, 0, 0)),
                  pl.BlockSpec((None, None, page, fw), pg),
                  pl.BlockSpec((None, None, page, fw), pg),
                  pl.BlockSpec((None, None, fh, page), pg),
                  full(knew), full(vnew), full(lfnew), full(zs)],
        out_specs=pl.BlockSpec((n, fw), lambda i, j, pt: (0, 0)),
        scratch_shapes=[pltpu.VMEM((V7X_SUBLANES, V7X_LANES), F32), pltpu.VMEM((V7X_SUBLANES, V7X_LANES), F32),
                        pltpu.VMEM((V7X_SUBLANES, fw), F32), pltpu.VMEM((fh, V7X_LANES), F32)],
    )
    return pl.pallas_call(
        functools.partial(_decode_kernel, fh=fh, fd=fd),
        out_shape=jax.ShapeDtypeStruct((n, fw), F32),
        grid_spec=grid_spec,
        compiler_params=_cparams(2),
        name="fox_sample",
    )(page_table, qblk, cache_k, cache_v, cache_lft, knew, vnew, lfnew, zs)


def _merge_kernel(ya_ref, yb_ref, yc_ref, ga_ref, gb_ref, gc_ref, x_ref, wpa_ref, wpb_ref, wpc_ref,
                  wout_ref, lng_ref, lnb_ref, xo_ref, xb_ref, *, alpha):
    def branch(y_ref, w_ref, g_ref):
        return _sigmoid(g_ref[...].astype(F32)) * _dot(y_ref[...].astype(BF16), w_ref[...])

    merged = branch(ya_ref, wpa_ref, ga_ref) + branch(yb_ref, wpb_ref, gb_ref) + branch(yc_ref, wpc_ref, gc_ref)
    out = _dot(merged.astype(BF16), wout_ref[...])
    r = alpha * x_ref[...] + out
    mu = jnp.mean(r, axis=-1, keepdims=True)
    var = jnp.mean(jnp.square(r - mu), axis=-1, keepdims=True)
    xn = (r - mu) * lax.rsqrt(var + LN_EPS) * lng_ref[...] + lnb_ref[...]
    xo_ref[...] = xn
    xb_ref[...] = xn.astype(BF16)


def _merge(ya, yb, yc, pm, x, wts, dm, layer, tm, alpha):
    m, d = x.shape
    sw, mw, fw = dm.sw, dm.mw, dm.fw
    row = lambda i: (i, 0)
    lay3 = lambda i: (layer, 0, 0)
    g0 = dm.off['gate'] // d
    return pl.pallas_call(
        functools.partial(_merge_kernel, alpha=alpha),
        out_shape=(jax.ShapeDtypeStruct((m, d), F32), jax.ShapeDtypeStruct((m, d), BF16)),
        grid=(m // tm,),
        in_specs=[pl.BlockSpec((tm, sw), row), pl.BlockSpec((tm, mw), row), pl.BlockSpec((tm, fw), row),
                  pl.BlockSpec((tm, d), lambda i: (i, g0)), pl.BlockSpec((tm, d), lambda i: (i, g0 + 1)),
                  pl.BlockSpec((tm, d), lambda i: (i, g0 + 2)),
                  pl.BlockSpec((tm, d), row),
                  pl.BlockSpec((None, sw, d), lay3), pl.BlockSpec((None, mw, d), lay3),
                  pl.BlockSpec((None, fw, d), lay3), pl.BlockSpec((None, d, d), lay3),
                  pl.BlockSpec((None, 1, d), lay3), pl.BlockSpec((None, 1, d), lay3)],
        out_specs=(pl.BlockSpec((tm, d), row), pl.BlockSpec((tm, d), row)),
        compiler_params=_cparams(1),
        name="merge",
    )(ya, yb, yc, pm, pm, pm, x, *wts)


def _prep_in_weights(w_in, b_in, dm):
    def seg(nm, scale=None):
        a, b = dm.src[nm]
        w, bb = w_in[:, :, a:b], b_in[:, a:b]
        if scale is not None:
            w, bb = w * scale, bb * scale
        return w, bb

    scales = {'m_k': dm.md ** -0.5, 'c_q': dm.fd ** -0.5}
    main = [seg(nm, scales.get(nm)) for nm in dm.main_order]
    w_main = jnp.concatenate([w for w, _ in main], axis=2).astype(BF16)
    b_main = jnp.concatenate([b for _, b in main], axis=1)[:, None, :]
    small = [seg('m_i'), seg('m_f'), seg('c_f')]
    pad = V7X_LANES - sum(w.shape[2] for w, _ in small)
    kv = [seg('c_k'), seg('c_v')] + small
    w_kv = jnp.concatenate([w for w, _ in kv] + [jnp.zeros(w_in.shape[:2] + (pad,), F32)], axis=2).astype(BF16)
    b_kv = jnp.concatenate([b for _, b in kv] + [jnp.zeros((w_in.shape[0], pad), F32)], axis=1)[:, None, :]
    return w_main, b_main, w_kv, b_kv


def _s5_mats(b_re, b_im, c_re, c_im, dm):
    g = dm.g
    eye = jnp.eye(g, dtype=F32)
    bm = lambda b: jnp.einsum('gpc,gh->gchp', b, eye).reshape(dm.sw, dm.gp)
    cm = lambda c: jnp.einsum('gcp,gh->gphc', c, eye).reshape(dm.gp, dm.sw)
    bmat = jnp.concatenate([bm(b_re), bm(b_im)], axis=1).astype(BF16)
    cmat = jnp.concatenate([cm(c_re), -cm(c_im)], axis=0).astype(BF16)
    return bmat, cmat


def kernel(x_prompt, x_sample, cache_k, cache_v, cache_logf, page_table, state_ssm_re, state_ssm_im,
           state_mlstm_c, state_mlstm_n, state_mlstm_m, w_in, b_in, ssm_a_re, ssm_a_im, ssm_b_re,
           ssm_b_im, ssm_c_re, ssm_c_im, ssm_d, ssm_log_dt, w_glu, b_glu, mlstm_norm_g, w_pa, w_pb,
           w_pc, w_out, ln_g, ln_b):
    dm = _Dims(x_prompt, x_sample, cache_k, page_table, state_mlstm_c, w_in, ssm_a_re, ssm_b_re)
    depth, d = dm.depth, dm.d
    nb, nd, seq = dm.batch, dm.dec_batch, dm.seq
    alpha = (2 * depth) ** 0.25

    w_main, b_main, w_kv, b_kv = _prep_in_weights(w_in, b_in, dm)
    merge_w = (w_pa.astype(BF16), w_pb.astype(BF16), w_pc.astype(BF16), w_out.astype(BF16),
               ln_g[:, None, :], ln_b[:, None, :])
    wglu_b = w_glu.astype(BF16)
    bglu3, d3, normg3 = b_glu[:, None, :], ssm_d[:, None, :], mlstm_norm_g[:, None, :]
    ck = cache_k.reshape(depth, dm.n_pool, dm.page, dm.fw)
    cv = cache_v.reshape(depth, dm.n_pool, dm.page, dm.fw)
    clft = jnp.swapaxes(cache_logf, 2, 3)
    page_table = page_table.astype(jnp.int32)
    head_eye = jnp.eye(dm.fh, V7X_LANES, dtype=F32)

    xp = x_prompt.reshape(nb * seq, d)
    xs = x_sample.reshape(nd, d)
    xpb, xsb = xp.astype(BF16), xs.astype(BF16)
    cf = slice(dm.lane_cf, dm.lane_cf + dm.fh)
    outs = {k: [] for k in ('kp', 'vp', 'lfp', 'ks', 'vs', 'lfs', 'srp', 'sip', 'srs', 'sis',
                            'mcp', 'mnp', 'mmp', 'mcs', 'mns', 'mms')}
    for l in range(depth):
        disc = _s5_disc(ssm_a_re[l], ssm_a_im[l], ssm_log_dt[l]).reshape(4, dm.gp)
        disc = jnp.concatenate([disc, jnp.zeros((V7X_SUBLANES - 4, dm.gp), F32)], axis=0)
        bmat, cmat = _s5_mats(ssm_b_re[l], ssm_b_im[l], ssm_c_re[l], ssm_c_im[l], dm)

        pm = _inproj_main(xpb, w_main, b_main, l, dm.tm_in, dm.tn_main, BF16)
        k32, v32, s32, kb, vb = _inproj_kv(xpb, w_kv, b_kv, l, dm.tm_in, dm.fw)
        lf, bc, fc, st, bct, fct = _gates(s32, nb, seq, dm.chunk)
        ya, sst = _s5_prompt(pm, dm, l, bmat, cmat, disc, d3, wglu_b, bglu3)
        yb, caug, mo = _mlstm_prompt(pm, bc, st, bct, normg3, dm, l)
        yc = _flash(pm, kb, vb, fc, fct, dm)
        xp, xpb = _merge(ya, yb, yc, pm, xp, merge_w, dm, l, dm.tm_merge, alpha)
        outs['kp'].append(k32)
        outs['vp'].append(v32)
        outs['lfp'].append(lf[:, cf])
        outs['srp'].append(sst[:, 0])
        outs['sip'].append(sst[:, 1])
        outs['mcp'].append(caug[..., :dm.md])
        outs['mnp'].append(caug[..., dm.md])
        outs['mmp'].append(mo[:, :dm.mh, 0])

        ps = _inproj_main(xsb, w_main, b_main, l, nd, dm.tn_main, F32)
        k32s, v32s, s32s, _, _ = _inproj_kv(xsb, w_kv, b_kv, l, nd, dm.fw)
        lfs = _logsig(s32s)
        yas, x1r, x1i = _s5_sample(ps, dm, l, state_ssm_re[l].reshape(nd, dm.gp),
                                   state_ssm_im[l].reshape(nd, dm.gp), bmat, cmat, disc, d3, wglu_b, bglu3)
        ybs, c1, n1, m1 = _mlstm_sample(ps, s32s, state_mlstm_c, state_mlstm_n, state_mlstm_m, normg3, dm, l)
        qs = ps[:, dm.off['c_q']:dm.off['c_q'] + dm.fw].reshape(nd, dm.fh, dm.fd)
        qblk = jnp.einsum('nhd,hg->nhdg', qs, head_eye).reshape(nd, dm.fw, V7X_LANES).astype(BF16)
        lfnew = jnp.pad(lfs[:, cf], ((0, 0), (0, V7X_LANES - dm.fh)))
        zs = ps[:, dm.off['c_z']:dm.off['c_z'] + dm.fw]
        ycs = _decode(page_table, qblk, ck, cv, clft, k32s, v32s, lfnew, zs, dm, l)
        xs, xsb = _merge(yas, ybs, ycs, ps, xs, merge_w, dm, l, nd, alpha)
        outs['ks'].append(k32s)
        outs['vs'].append(v32s)
        outs['lfs'].append(lfs[:, cf])
        outs['srs'].append(x1r)
        outs['sis'].append(x1i)
        outs['mcs'].append(c1)
        outs['mns'].append(n1)
        outs['mms'].append(m1)

    stk = lambda k, shape: jnp.stack(outs[k]).reshape((depth,) + shape)
    return (xp.reshape(nb, seq, d), xs.reshape(nd, 1, d),
            stk('kp', (nb, seq, dm.fh, dm.fd)), stk('vp', (nb, seq, dm.fh, dm.fd)), stk('lfp', (nb, seq, dm.fh)),
            stk('ks', (nd, 1, dm.fh, dm.fd)), stk('vs', (nd, 1, dm.fh, dm.fd)), stk('lfs', (nd, 1, dm.fh)),
            stk('srp', (nb, dm.g, dm.p)), stk('sip', (nb, dm.g, dm.p)),
            stk('srs', (nd, dm.g, dm.p)), stk('sis', (nd, dm.g, dm.p)),
            stk('mcp', (nb, dm.mh, dm.md, dm.md)), stk('mnp', (nb, dm.mh, dm.md)), stk('mmp', (nb, dm.mh)),
            stk('mcs', (nd, dm.mh, dm.md, dm.md)), stk('mns', (nd, dm.mh, dm.md)), stk('mms', (nd, dm.mh)))
```

```python
import functools
import math

import jax
import jax.numpy as jnp
from jax import lax
from jax.experimental import pallas as pl
from jax.experimental.pallas import tpu as pltpu

F32 = jnp.float32
BF16 = jnp.bfloat16

V7X_LANES = 128
V7X_SUBLANES = 8
V7X_VMEM_LIMIT_BYTES = 56 * 1024 * 1024

LN_EPS = 1e-5
NEG_INF = float("-inf")
LOG2E = math.log2(math.e)
V7X_MXU_DIM = 256


def _cparams(n_axes):
    return pltpu.CompilerParams(dimension_semantics=("arbitrary",) * n_axes,
                                vmem_limit_bytes=V7X_VMEM_LIMIT_BYTES)


def _sigmoid(x):
    return 1.0 / (1.0 + jnp.exp(-x))


def _silu(x):
    return x * _sigmoid(x)


def _log_sigmoid(x):
    return jnp.minimum(x, 0.0) - jnp.log1p(jnp.exp(-jnp.abs(x)))


def _gelu_tanh(x):
    c = math.sqrt(2.0 / math.pi)
    return 0.5 * x * (1.0 + jnp.tanh(c * (x + 0.044715 * (x * x * x))))


def _dot(a, b):
    return jnp.dot(a, b, preferred_element_type=F32)


def _dot_nt(a, b):
    return lax.dot_general(a, b, (((1,), (1,)), ((), ())), preferred_element_type=F32)


def _split3(x):
    hi = x.astype(BF16)
    r1 = x - hi.astype(F32)
    mid = r1.astype(BF16)
    lo = (r1 - mid.astype(F32)).astype(BF16)
    return hi, mid, lo


def _cumsum_rows(x):
    n = x.shape[0]
    r = lax.broadcasted_iota(jnp.int32, (n, n), 0)
    c = lax.broadcasted_iota(jnp.int32, (n, n), 1)
    tri = (r >= c).astype(BF16)
    hi, mid, lo = _split3(x)
    return _dot(tri, hi) + _dot(tri, mid) + _dot(tri, lo)


def _cumsum_lanes(x):
    n = x.shape[1]
    r = lax.broadcasted_iota(jnp.int32, (n, n), 0)
    c = lax.broadcasted_iota(jnp.int32, (n, n), 1)
    tri = (r <= c).astype(BF16)
    hi, mid, lo = _split3(x)
    return _dot(hi, tri) + _dot(mid, tri) + _dot(lo, tri)


class _Dims:
    def __init__(self, x_prompt, x_sample, cache_k, page_table, state_mlstm_c, w_in, ssm_a_re, ssm_b_re):
        self.batch, self.seq, self.d = x_prompt.shape
        self.dec_batch = x_sample.shape[0]
        assert x_sample.shape[1] == 1
        self.depth = w_in.shape[0]
        self.g, self.p = ssm_a_re.shape[1], ssm_a_re.shape[2]
        self.gs = ssm_b_re.shape[3]
        self.sw = self.g * self.gs
        self.gp = self.g * self.p
        self.mh, self.md = state_mlstm_c.shape[2], state_mlstm_c.shape[3]
        self.mw = self.mh * self.md
        self.page, self.fh, self.fd = cache_k.shape[2], cache_k.shape[3], cache_k.shape[4]
        self.fw = self.fh * self.fd
        self.n_pages = page_table.shape[1]
        self.n_pool = cache_k.shape[1]
        d, sw, mw, fw, mh, fh = self.d, self.sw, self.mw, self.fw, self.mh, self.fh
        widths = (('ssm_u', sw), ('ssm_z', sw), ('m_q', mw), ('m_k', mw), ('m_v', mw), ('m_i', mh),
                  ('m_f', mh), ('m_o', mw), ('m_z', mw), ('c_q', fw), ('c_k', fw), ('c_v', fw),
                  ('c_f', fh), ('c_z', fw), ('gate', 3 * d))
        self.src = {}
        off = 0
        for nm, wd in widths:
            self.src[nm] = (off, off + wd)
            off += wd
        assert off == w_in.shape[2]
        self.main_order = ('gate', 'ssm_u', 'ssm_z', 'm_q', 'm_k', 'm_v', 'm_o', 'm_z', 'c_q', 'c_z')
        self.off = {}
        off = 0
        for nm in self.main_order:
            self.off[nm] = off
            off += self.src[nm][1] - self.src[nm][0]
        self.n_main = off
        self.n_kv = 2 * fw + V7X_LANES
        self.lane_li, self.lane_lf, self.lane_cf = 0, mh, 2 * mh
        assert 2 * mh + fh <= V7X_LANES
        assert self.md % V7X_LANES == 0 and self.fd % V7X_LANES == 0 and sw % V7X_LANES == 0
        assert fh == V7X_SUBLANES and self.page == V7X_LANES
        for nm in ('ssm_u', 'ssm_z', 'm_q', 'm_k', 'm_v', 'm_o', 'm_z'):
            assert self.off[nm] % sw == 0 and self.off[nm] % mw == 0
        assert self.off['c_q'] % self.fd == 0 and self.off['c_z'] % self.fd == 0
        self.m_rows = self.batch * self.seq
        self.tm_in = min(512, self.m_rows)
        self.tn_main = max(t for t in range(V7X_LANES, 3072 + 1, V7X_LANES) if self.n_main % t == 0)
        self.chunk = min(256, self.seq)
        self.s5_chunk = min(512, self.seq)
        self.tq = min(512, self.seq)
        self.tk = self.tq
        assert self.tq % self.tk == 0
        self.tm_merge = min(256, self.m_rows)
        self.pages_per_step = 4 if self.n_pages % 4 == 0 else 1
        self.flash_heads = 2 if (fh % 2 == 0 and self.off['c_q'] % (2 * self.fd) == 0
                                 and self.off['c_z'] % (2 * self.fd) == 0) else 1
        self.tr_repack = min(128, self.d)
        assert self.seq % self.chunk == 0 and self.seq % self.tq == 0 and self.seq % self.s5_chunk == 0
        assert self.m_rows % self.tm_in == 0 and self.m_rows % self.tm_merge == 0
        assert self.d % self.tr_repack == 0


def _mm_bias_kernel(x_ref, w_ref, b_ref, o_ref):
    acc = _dot(x_ref[...].astype(BF16), w_ref[...]) + b_ref[...]
    o_ref[...] = acc.astype(o_ref.dtype)


def _inproj_main(xb, w_all, b_all, layer, tm, tn, out_dtype):
    m, d = xb.shape
    n = w_all.shape[2]
    return pl.pallas_call(
        _mm_bias_kernel,
        out_shape=jax.ShapeDtypeStruct((m, n), out_dtype),
        grid=(n // tn, m // tm),
        in_specs=[pl.BlockSpec((tm, d), lambda j, i: (i, 0)),
                  pl.BlockSpec((None, d, tn), lambda j, i: (layer, 0, j)),
                  pl.BlockSpec((None, 1, tn), lambda j, i: (layer, 0, j))],
        out_specs=pl.BlockSpec((tm, tn), lambda j, i: (i, j)),
        compiler_params=_cparams(2),
        name="inproj_main",
    )(xb, w_all, b_all)


def _kv_kernel(x_ref, w_ref, b_ref, ws_ref, bs_ref, k32_ref, v32_ref, s32_ref, kb_ref, vb_ref, *, fw):
    x = x_ref[...].astype(BF16)
    acc = _dot(x, w_ref[...]) + b_ref[...]
    k = acc[:, :fw]
    v = acc[:, fw:]
    k32_ref[...] = k
    v32_ref[...] = v
    s32_ref[...] = _dot(x, ws_ref[...]) + bs_ref[...]
    kb_ref[...] = k.astype(BF16)
    vb_ref[...] = v.astype(BF16)


def _inproj_kv(xb, w_kv, b_kv, w_small, b_small, layer, tm, fw):
    m, d = xb.shape
    row = lambda i: (i, 0)
    lay3 = lambda i: (layer, 0, 0)
    return pl.pallas_call(
        functools.partial(_kv_kernel, fw=fw),
        out_shape=(jax.ShapeDtypeStruct((m, fw), F32), jax.ShapeDtypeStruct((m, fw), F32),
                   jax.ShapeDtypeStruct((m, V7X_LANES), F32),
                   jax.ShapeDtypeStruct((m, fw), BF16), jax.ShapeDtypeStruct((m, fw), BF16)),
        grid=(m // tm,),
        in_specs=[pl.BlockSpec((tm, d), row),
                  pl.BlockSpec((None, d, 2 * fw), lay3), pl.BlockSpec((None, 1, 2 * fw), lay3),
                  pl.BlockSpec((None, d, V7X_LANES), lay3), pl.BlockSpec((None, 1, V7X_LANES), lay3)],
        out_specs=(pl.BlockSpec((tm, fw), row), pl.BlockSpec((tm, fw), row),
                   pl.BlockSpec((tm, V7X_LANES), row),
                   pl.BlockSpec((tm, fw), row), pl.BlockSpec((tm, fw), row)),
        compiler_params=_cparams(1),
        name="inproj_kv",
    )(xb, w_kv, b_kv, w_small, b_small)


def _repack_kernel(w_ref, om_ref, okv_ref, *, plan_main, plan_kv):
    for o_ref, plan in ((om_ref, plan_main), (okv_ref, plan_kv)):
        for a, b, dst, scale in plan:
            w = w_ref[:, a:b]
            if scale != 1.0:
                w = w * scale
            o_ref[:, dst:dst + (b - a)] = w.astype(BF16)


def _repack(w_in, dm, plan_main, plan_kv):
    depth, d, n_in = w_in.shape
    tr = dm.tr_repack
    blk = lambda n: pl.BlockSpec((None, tr, n), lambda l, r: (l, r, 0))
    return pl.pallas_call(
        functools.partial(_repack_kernel, plan_main=plan_main, plan_kv=plan_kv),
        out_shape=(jax.ShapeDtypeStruct((depth, d, dm.n_main), BF16),
                   jax.ShapeDtypeStruct((depth, d, 2 * dm.fw), BF16)),
        grid=(depth, d // tr),
        in_specs=[blk(n_in)],
        out_specs=(blk(dm.n_main), blk(2 * dm.fw)),
        compiler_params=_cparams(2),
        name="repack_w_in",
    )(w_in)


def _gates_kernel(s_ref, lf_ref, bc_ref, st_ref, bct_ref, qx_ref, kx_ref, carry_ref, *, fh, lane_cf):
    @pl.when(pl.program_id(1) == 0)
    def _():
        carry_ref[...] = jnp.zeros_like(carry_ref)

    s = s_ref[...]
    lf = _log_sigmoid(s)
    bc = _cumsum_rows(lf)
    fc = bc + carry_ref[0:1, :]
    carry_ref[...] = jnp.broadcast_to(fc[-1:, :], carry_ref.shape)
    lf_ref[...] = lf
    bc_ref[...] = bc
    st_ref[...] = s.T
    bct_ref[...] = bc.T
    lane = lax.broadcasted_iota(jnp.int32, s.shape, 1)
    for h in range(fh):
        hi, mid, lo = (t.astype(F32) for t in _split3(fc[:, lane_cf + h:lane_cf + h + 1] * LOG2E))
        qx = jnp.where(lane == 0, hi, jnp.where(lane == 1, mid, jnp.where(lane == 2, lo,
                       jnp.where(lane < 6, 1.0, 0.0))))
        kx = jnp.where(lane < 3, 1.0, jnp.where(lane == 3, -hi, jnp.where(lane == 4, -mid,
                       jnp.where(lane == 5, -lo, 0.0))))
        hs = slice(h * V7X_LANES, (h + 1) * V7X_LANES)
        qx_ref[:, hs] = qx.astype(BF16)
        kx_ref[:, hs] = kx.astype(BF16)


def _gates(s32, n_seq, seq, chunk, fh, lane_cf):
    m = s32.shape[0]
    nc = seq // chunk
    row = pl.BlockSpec((chunk, V7X_LANES), lambda n, c: (n * nc + c, 0))
    tr = pl.BlockSpec((None, V7X_LANES, chunk), lambda n, c: (n, 0, c))
    xrow = pl.BlockSpec((chunk, fh * V7X_LANES), lambda n, c: (n * nc + c, 0))
    rs = jax.ShapeDtypeStruct((m, V7X_LANES), F32)
    ts = jax.ShapeDtypeStruct((n_seq, V7X_LANES, seq), F32)
    xs = jax.ShapeDtypeStruct((m, fh * V7X_LANES), BF16)
    return pl.pallas_call(
        functools.partial(_gates_kernel, fh=fh, lane_cf=lane_cf),
        out_shape=(rs, rs, ts, ts, xs, xs),
        grid=(n_seq, nc),
        in_specs=[row],
        out_specs=(row, row, tr, tr, xrow, xrow),
        scratch_shapes=[pltpu.VMEM((V7X_SUBLANES, V7X_LANES), F32)],
        compiler_params=_cparams(2),
        name="gates",
    )(s32)


def _logsig_kernel(s_ref, o_ref):
    o_ref[...] = _log_sigmoid(s_ref[...])


def _logsig(s32):
    return pl.pallas_call(_logsig_kernel, out_shape=jax.ShapeDtypeStruct(s32.shape, F32), name="logsig")(s32)


def _s5_disc_kernel(are_ref, aim_ref, ldt_ref, o_ref):
    a_re, a_im = are_ref[...], aim_ref[...]
    dt = jnp.exp(ldt_ref[...])
    mag = jnp.exp(a_re * dt)
    ab_re, ab_im = mag * jnp.cos(a_im * dt), mag * jnp.sin(a_im * dt)
    nr, ni = ab_re - 1.0, ab_im
    den = a_re * a_re + a_im * a_im
    o_ref[0] = ab_re
    o_ref[1] = ab_im
    o_ref[2] = (nr * a_re + ni * a_im) / den
    o_ref[3] = (ni * a_re - nr * a_im) / den


def _s5_disc(a_re, a_im, log_dt):
    g, p = a_re.shape
    ldt = jnp.broadcast_to(log_dt[:, None], (g, p))
    return pl.pallas_call(_s5_disc_kernel, out_shape=jax.ShapeDtypeStruct((4, g, p), F32),
                          name="s5_disc")(a_re, a_im, ldt)


def _s5_in_map(u, bmat_ref, gp, sw):
    kt = min(V7X_MXU_DIM, sw)
    ct = gp * kt // sw
    re, im = [], []
    for i in range(sw // kt):
        uk = u[:, i * kt:(i + 1) * kt]
        re.append(_dot(uk, bmat_ref[i * kt:(i + 1) * kt, i * ct:(i + 1) * ct]))
        im.append(_dot(uk, bmat_ref[i * kt:(i + 1) * kt, gp + i * ct:gp + (i + 1) * ct]))
    return jnp.concatenate(re, axis=1), jnp.concatenate(im, axis=1)


def _s5_out_map(xr, xi, cmat_ref, gp, sw):
    nt = min(V7X_MXU_DIM, sw)
    ct = gp * nt // sw
    ys = []
    for i in range(sw // nt):
        cs, ns = slice(i * ct, (i + 1) * ct), slice(i * nt, (i + 1) * nt)
        ys.append(_dot(xr[:, cs], cmat_ref[i * ct:(i + 1) * ct, ns])
                  + _dot(xi[:, cs], cmat_ref[gp + i * ct:gp + (i + 1) * ct, ns]))
    return jnp.concatenate(ys, axis=1)


def _s5_tail(y_state, u, z, dvec, wglu, bglu):
    gact = _gelu_tanh(y_state + dvec * u)
    glu = _dot(gact.astype(BF16), wglu) + bglu
    return gact * _sigmoid(glu) * _silu(z)


def _s5_prompt_kernel(u_ref, z_ref, bmat_ref, cmat_ref, disc_ref, d_ref, wglu_ref, bglu_ref,
                      ya_ref, st_ref, buf_ref, carry_ref, *, gp, sw):
    c = pl.program_id(1)

    @pl.when(c == 0)
    def _():
        carry_ref[...] = jnp.zeros_like(carry_ref)

    u = u_ref[...]
    br, bi = _s5_in_map(u, bmat_ref, gp, sw)
    ar, ai = disc_ref[0:1, :], disc_ref[1:2, :]
    kr, ki = disc_ref[2:3, :], disc_ref[3:4, :]
    buf_ref[:, :gp] = kr * br - ki * bi
    buf_ref[:, gp:] = kr * bi + ki * br

    def step(t, carry):
        xr, xi = carry
        nr = ar * xr - ai * xi + buf_ref[pl.ds(t, 1), :gp]
        ni = ar * xi + ai * xr + buf_ref[pl.ds(t, 1), gp:]
        buf_ref[pl.ds(t, 1), :gp] = nr
        buf_ref[pl.ds(t, 1), gp:] = ni
        return nr, ni

    xr, xi = lax.fori_loop(0, u.shape[0], step, (carry_ref[0:1, :], carry_ref[1:2, :]))
    carry_ref[0:1, :] = xr
    carry_ref[1:2, :] = xi
    y_state = _s5_out_map(buf_ref[:, :gp].astype(BF16), buf_ref[:, gp:].astype(BF16), cmat_ref, gp, sw)
    ya = _s5_tail(y_state, u.astype(F32), z_ref[...].astype(F32), d_ref[...], wglu_ref[...], bglu_ref[...])
    ya_ref[...] = ya.astype(ya_ref.dtype)

    @pl.when(c == pl.num_programs(1) - 1)
    def _():
        st_ref[...] = carry_ref[...]


def _s5_prompt(pm, dm, layer, bmat, cmat, disc, dvec, wglu, bglu):
    tc = dm.s5_chunk
    nc = dm.seq // tc
    sw, gp = dm.sw, dm.gp
    cu, cz = dm.off['ssm_u'] // sw, dm.off['ssm_z'] // sw
    const2 = lambda n, c: (0, 0)
    lay3 = lambda n, c: (layer, 0, 0)
    return pl.pallas_call(
        functools.partial(_s5_prompt_kernel, gp=gp, sw=sw),
        out_shape=(jax.ShapeDtypeStruct((dm.m_rows, sw), BF16),
                   jax.ShapeDtypeStruct((dm.batch, V7X_SUBLANES, gp), F32)),
        grid=(dm.batch, nc),
        in_specs=[pl.BlockSpec((tc, sw), lambda n, c: (n * nc + c, cu)),
                  pl.BlockSpec((tc, sw), lambda n, c: (n * nc + c, cz)),
                  pl.BlockSpec(bmat.shape, const2), pl.BlockSpec(cmat.shape, const2),
                  pl.BlockSpec(disc.shape, const2),
                  pl.BlockSpec((None, 1, sw), lay3), pl.BlockSpec((None, sw, sw), lay3),
                  pl.BlockSpec((None, 1, sw), lay3)],
        out_specs=(pl.BlockSpec((tc, sw), lambda n, c: (n * nc + c, 0)),
                   pl.BlockSpec((None, V7X_SUBLANES, gp), lambda n, c: (n, 0, 0))),
        scratch_shapes=[pltpu.VMEM((tc, 2 * gp), F32), pltpu.VMEM((V7X_SUBLANES, gp), F32)],
        compiler_params=_cparams(2),
        name="s5_prompt",
    )(pm, pm, bmat, cmat, disc, dvec, wglu, bglu)


def _s5_sample_kernel(u_ref, z_ref, x0r_ref, x0i_ref, bmat_ref, cmat_ref, disc_ref, d_ref, wglu_ref,
                      bglu_ref, ya_ref, x1r_ref, x1i_ref, *, gp, sw):
    u = u_ref[...]
    br, bi = _s5_in_map(u.astype(BF16), bmat_ref, gp, sw)
    ar, ai = disc_ref[0:1, :], disc_ref[1:2, :]
    kr, ki = disc_ref[2:3, :], disc_ref[3:4, :]
    x0r, x0i = x0r_ref[...], x0i_ref[...]
    x1r = (kr * br - ki * bi) + (ar * x0r - ai * x0i)
    x1i = (kr * bi + ki * br) + (ar * x0i + ai * x0r)
    x1r_ref[...] = x1r
    x1i_ref[...] = x1i
    y_state = _s5_out_map(x1r.astype(BF16), x1i.astype(BF16), cmat_ref, gp, sw)
    ya_ref[...] = _s5_tail(y_state, u, z_ref[...], d_ref[0], wglu_ref[0], bglu_ref[0])


def _s5_sample(ps, dm, layer, x0r, x0i, bmat, cmat, disc, dvec, wglu, bglu):
    n, sw, gp = dm.dec_batch, dm.sw, dm.gp
    cu, cz = dm.off['ssm_u'] // sw, dm.off['ssm_z'] // sw
    full = lambda a: pl.BlockSpec(a.shape, lambda i: (0,) * a.ndim)
    lay3 = lambda i: (layer, 0, 0)
    return pl.pallas_call(
        functools.partial(_s5_sample_kernel, gp=gp, sw=sw),
        out_shape=(jax.ShapeDtypeStruct((n, sw), F32), jax.ShapeDtypeStruct((n, gp), F32),
                   jax.ShapeDtypeStruct((n, gp), F32)),
        grid=(1,),
        in_specs=[pl.BlockSpec((n, sw), lambda i: (0, cu)), pl.BlockSpec((n, sw), lambda i: (0, cz)),
                  full(x0r), full(x0i), full(bmat), full(cmat), full(disc),
                  pl.BlockSpec((1, 1, sw), lay3), pl.BlockSpec((1, sw, sw), lay3),
                  pl.BlockSpec((1, 1, sw), lay3)],
        out_specs=(pl.BlockSpec((n, sw), lambda i: (0, 0)), pl.BlockSpec((n, gp), lambda i: (0, 0)),
                   pl.BlockSpec((n, gp), lambda i: (0, 0))),
        compiler_params=_cparams(1),
        name="s5_sample",
    )(ps, ps, x0r, x0i, bmat, cmat, disc, dvec, wglu, bglu)


def _head_norm(h):
    mu = jnp.mean(h, axis=-1, keepdims=True)
    var = jnp.mean(jnp.square(h - mu), axis=-1, keepdims=True)
    return (h - mu) * lax.rsqrt(var + LN_EPS)


def _mlstm_prompt_kernel(q_ref, k_ref, v_ref, o_ref, z_ref, bc_ref, st_ref, bct_ref, g_ref,
                         yb_ref, cout_ref, mout_ref, cst_ref, mst_ref, *, mh, md, lane_li, lane_lf):
    c = pl.program_id(1)

    @pl.when(c == 0)
    def _():
        cst_ref[...] = jnp.zeros_like(cst_ref)
        mst_ref[...] = jnp.zeros_like(mst_ref)

    ln = q_ref.shape[0]
    row = lax.broadcasted_iota(jnp.int32, (ln, ln), 0)
    col = lax.broadcasted_iota(jnp.int32, (ln, ln), 1)
    causal = row >= col
    ones_col = (lax.broadcasted_iota(jnp.int32, (ln, V7X_LANES), 1) == 0).astype(BF16)
    for h in range(mh):
        hs = slice(h * md, (h + 1) * md)
        q, k, v = q_ref[:, hs], k_ref[:, hs], v_ref[:, hs]
        a_row = st_ref[lane_li + h:lane_li + h + 1, :] - bct_ref[lane_lf + h:lane_lf + h + 1, :]
        b_col = bc_ref[:, lane_lf + h:lane_lf + h + 1]
        m0 = mst_ref[h:h + 1, 0:1]
        amat = jnp.where(causal, a_row, NEG_INF)
        m_col = jnp.maximum(m0, jnp.max(amat, axis=1, keepdims=True))
        w = jnp.exp(amat - m_col)
        s_inter = jnp.exp(m0 - m_col)
        qk = (_dot_nt(q, k) * w).astype(BF16)
        v_aug = jnp.concatenate([v, ones_col], axis=1)
        c0 = cst_ref[h]
        num_aug = s_inter * _dot(q, c0.astype(BF16)) + _dot(qk, v_aug)
        num, den = num_aug[:, :md], num_aug[:, md:md + 1]
        hh = num / jnp.maximum(jnp.abs(den), jnp.exp(-(b_col + m_col)))
        yb = (_head_norm(hh) * g_ref[:, hs] * _sigmoid(o_ref[:, hs].astype(F32))
              * _silu(z_ref[:, hs].astype(F32)))
        yb_ref[:, hs] = yb.astype(yb_ref.dtype)
        m_end = m_col[ln - 1:ln, :]
        decay = jnp.exp(m0 - m_end)
        ws_row = jnp.exp(a_row - m_end)
        kw = (k.astype(F32).T * ws_row).astype(BF16)
        cst_ref[h] = decay * c0 + _dot(kw, v_aug)
        mst_ref[h:h + 1, :] = jnp.broadcast_to(b_col[ln - 1:ln, :] + m_end, (1, V7X_LANES))

    @pl.when(c == pl.num_programs(1) - 1)
    def _():
        cout_ref[...] = cst_ref[...]
        mout_ref[...] = mst_ref[...]


def _mlstm_prompt(pm, bc, st, bct, norm_g, dm, layer):
    ln = dm.chunk
    nc = dm.seq // ln
    mw, mh, md = dm.mw, dm.mh, dm.md
    aug = md + V7X_LANES
    rowblk = lambda nm: pl.BlockSpec((ln, mw), lambda n, c, o=dm.off[nm] // mw: (n * nc + c, o))
    tr = pl.BlockSpec((None, V7X_LANES, ln), lambda n, c: (n, 0, c))
    return pl.pallas_call(
        functools.partial(_mlstm_prompt_kernel, mh=mh, md=md, lane_li=dm.lane_li, lane_lf=dm.lane_lf),
        out_shape=(jax.ShapeDtypeStruct((dm.m_rows, mw), BF16),
                   jax.ShapeDtypeStruct((dm.batch, mh, md, aug), F32),
                   jax.ShapeDtypeStruct((dm.batch, V7X_SUBLANES, V7X_LANES), F32)),
        grid=(dm.batch, nc),
        in_specs=[rowblk('m_q'), rowblk('m_k'), rowblk('m_v'), rowblk('m_o'), rowblk('m_z'),
                  pl.BlockSpec((ln, V7X_LANES), lambda n, c: (n * nc + c, 0)), tr, tr,
                  pl.BlockSpec((None, 1, mw), lambda n, c: (layer, 0, 0))],
        out_specs=(pl.BlockSpec((ln, mw), lambda n, c: (n * nc + c, 0)),
                   pl.BlockSpec((None, mh, md, aug), lambda n, c: (n, 0, 0, 0)),
                   pl.BlockSpec((None, V7X_SUBLANES, V7X_LANES), lambda n, c: (n, 0, 0))),
        scratch_shapes=[pltpu.VMEM((mh, md, aug), F32), pltpu.VMEM((V7X_SUBLANES, V7X_LANES), F32)],
        compiler_params=_cparams(2),
        name="mlstm_prompt",
    )(pm, pm, pm, pm, pm, bc, st, bct, norm_g)


def _mlstm_sample_kernel(ps_ref, s_ref, c0_ref, n0_ref, m0_ref, g_ref, yb_ref, c1_ref, n1_ref, m1_ref,
                         *, nseq, mh, md, offs, lane_li, lane_lf):
    s = s_ref[...]
    lfa = _log_sigmoid(s)
    eye = (lax.broadcasted_iota(jnp.int32, (md, md), 0) == lax.broadcasted_iota(jnp.int32, (md, md), 1))
    for n in range(nseq):
        for h in range(mh):
            def seg(nm):
                o = offs[nm] + h * md
                return ps_ref[n:n + 1, o:o + md]
            q = seg('m_q').astype(BF16)
            k = seg('m_k').astype(BF16)
            v = seg('m_v').astype(BF16).astype(F32)
            li = s[n:n + 1, lane_li + h:lane_li + h + 1]
            lf = lfa[n:n + 1, lane_lf + h:lane_lf + h + 1]
            m0 = m0_ref[n:n + 1, h:h + 1]
            c0 = c0_ref[n, h]
            n0 = n0_ref[n, h:h + 1, :]
            inter = lf + m0
            m1 = jnp.maximum(inter, li)
            w = jnp.exp(li - m1)
            s_inter = jnp.exp(inter - m1)
            kf = k.astype(F32)
            qk = jnp.sum(q.astype(F32) * kf, axis=1, keepdims=True) * w
            qc = _dot(jnp.broadcast_to(q, (V7X_SUBLANES, md)), c0.astype(BF16))[0:1, :]
            num = s_inter * qc + qk * v
            den = s_inter * jnp.sum(q.astype(F32) * n0, axis=1, keepdims=True) + qk
            hh = num / jnp.maximum(jnp.abs(den), jnp.exp(-m1))
            go = g_ref[0, :, h * md:(h + 1) * md]
            yb = _head_norm(hh) * go * _sigmoid(seg('m_o')) * _silu(seg('m_z'))
            yb_ref[n:n + 1, h * md:(h + 1) * md] = yb
            k_col = jnp.sum(jnp.where(eye, kf, 0.0), axis=1, keepdims=True)
            c1_ref[n, h] = s_inter * c0 + (w * k_col) * v
            n1_ref[n, h:h + 1, :] = s_inter * n0 + w * kf
            m1_ref[n:n + 1, h:h + 1] = m1


def _mlstm_sample(ps, s32, c0, n0, m0, norm_g, dm, layer):
    n, mh, md, mw = dm.dec_batch, dm.mh, dm.md, dm.mw
    offs = {nm: dm.off[nm] for nm in ('m_q', 'm_k', 'm_v', 'm_o', 'm_z')}
    full = lambda a: pl.BlockSpec(a.shape, lambda i: (0,) * a.ndim)
    st4 = lambda i: (layer, 0, 0, 0, 0)
    return pl.pallas_call(
        functools.partial(_mlstm_sample_kernel, nseq=n, mh=mh, md=md, offs=offs, lane_li=dm.lane_li,
                          lane_lf=dm.lane_lf),
        out_shape=(jax.ShapeDtypeStruct((n, mw), F32), jax.ShapeDtypeStruct((n, mh, md, md), F32),
                   jax.ShapeDtypeStruct((n, mh, md), F32), jax.ShapeDtypeStruct((n, mh), F32)),
        grid=(1,),
        in_specs=[full(ps), full(s32),
                  pl.BlockSpec((None, n, mh, md, md), st4),
                  pl.BlockSpec((None, n, mh, md), lambda i: (layer, 0, 0, 0)),
                  pl.BlockSpec((None, n, mh), lambda i: (layer, 0, 0)),
                  pl.BlockSpec((1, 1, mw), lambda i: (layer, 0, 0))],
        out_specs=(pl.BlockSpec((n, mw), lambda i: (0, 0)),
                   pl.BlockSpec((n, mh, md, md), lambda i: (0, 0, 0, 0)),
                   pl.BlockSpec((n, mh, md), lambda i: (0, 0, 0)),
                   pl.BlockSpec((n, mh), lambda i: (0, 0))),
        compiler_params=_cparams(1),
        name="mlstm_sample",
    )(ps, s32, c0, n0, m0, norm_g)


def _flash_kernel(q_ref, qx_ref, z_ref, k_ref, kx_ref, v_ref, yc_ref, *, tq, tk, fd, heads):
    qb = pl.program_id(2)
    hsl = [slice(h * fd, (h + 1) * fd) for h in range(heads)]
    xsl = [slice(h * V7X_LANES, (h + 1) * V7X_LANES) for h in range(heads)]
    qs = [jnp.concatenate([q_ref[:, hsl[h]], qx_ref[:, xsl[h]]], axis=1) for h in range(heads)]
    row = lax.broadcasted_iota(jnp.int32, (tq, tk), 0)
    col = lax.broadcasted_iota(jnp.int32, (tq, tk), 1)

    def block(kb, carry, diag_offset=None):
        ks = pl.multiple_of(kb * tk, tk)
        out = []
        for h in range(heads):
            m_i, l_i, acc = carry[h]
            k = jnp.concatenate([k_ref[pl.ds(ks, tk), hsl[h]], kx_ref[pl.ds(ks, tk), xsl[h]]], axis=1)
            s = _dot_nt(qs[h], k)
            if diag_offset is not None:
                s = jnp.where(row >= col + diag_offset, s, NEG_INF)
            m_new = jnp.maximum(m_i, jnp.max(s, axis=1, keepdims=True))
            alpha = jnp.exp2(m_i - m_new)
            p = jnp.exp2(s - m_new)
            l_new = alpha * l_i + jnp.sum(p, axis=1, keepdims=True)
            acc_new = alpha * acc + _dot(p.astype(BF16), v_ref[pl.ds(ks, tk), hsl[h]])
            out.append((m_new, l_new, acc_new))
        return tuple(out)

    init = tuple((jnp.full((tq, 1), NEG_INF, F32), jnp.zeros((tq, 1), F32), jnp.zeros((tq, fd), F32))
                 for _ in range(heads))
    per_q = tq // tk
    fin = lax.fori_loop(0, qb * per_q, block, init)
    for j in range(per_q):
        fin = block(qb * per_q + j, fin, j * tk)
    for h in range(heads):
        _, l_f, acc = fin[h]
        yc_ref[:, hsl[h]] = ((acc / l_f) * _silu(z_ref[:, hsl[h]].astype(F32))).astype(yc_ref.dtype)


def _flash(pm, kb, vb, qx, kx, dm):
    tq, fd, fh, seq = dm.tq, dm.fd, dm.fh, dm.seq
    nq = seq // tq
    hp = dm.flash_heads
    cq, cz = dm.off['c_q'] // (hp * fd), dm.off['c_z'] // (hp * fd)
    qrow = lambda c0: (lambda n, h, i: (n * nq + i, c0 + h))
    whole = lambda n, h, i: (n, h)
    return pl.pallas_call(
        functools.partial(_flash_kernel, tq=tq, tk=dm.tk, fd=fd, heads=hp),
        out_shape=jax.ShapeDtypeStruct((dm.m_rows, dm.fw), BF16),
        grid=(dm.batch, fh // hp, nq),
        in_specs=[pl.BlockSpec((tq, hp * fd), qrow(cq)), pl.BlockSpec((tq, hp * V7X_LANES), qrow(0)),
                  pl.BlockSpec((tq, hp * fd), qrow(cz)),
                  pl.BlockSpec((seq, hp * fd), whole), pl.BlockSpec((seq, hp * V7X_LANES), whole),
                  pl.BlockSpec((seq, hp * fd), whole)],
        out_specs=pl.BlockSpec((tq, hp * fd), qrow(0)),
        compiler_params=_cparams(3),
        name="fox_prompt",
    )(pm, qx, pm, kb, kx, vb)


def _decode_kernel(pt_ref, q_ref, knew_ref, vnew_ref, lfnew_ref, z_ref, *rest, pages_per_step):
    pb = pages_per_step
    k_refs, v_refs, lft_refs = rest[:pb], rest[pb:2 * pb], rest[2 * pb:3 * pb]
    o_ref, m_ref, l_ref, acc_ref, cs_ref = rest[3 * pb:]
    j = pl.program_id(1)
    page = k_refs[0].shape[0]

    @pl.when(j == 0)
    def _():
        m_ref[...] = jnp.full_like(m_ref, NEG_INF)
        l_ref[...] = jnp.zeros_like(l_ref)
        acc_ref[...] = jnp.zeros_like(acc_ref)
        cs_ref[...] = jnp.zeros_like(cs_ref)

    q = q_ref[...]
    fh, fd = q.shape
    ones = jnp.ones((fd, V7X_LANES), BF16)
    diag = (lax.broadcasted_iota(jnp.int32, (page, fh, V7X_LANES), 0)
            == lax.broadcasted_iota(jnp.int32, (page, fh, V7X_LANES), 2))
    cum_all = _cumsum_lanes(jnp.concatenate([r[...] for r in lft_refs], axis=0))
    m_run, base = m_ref[:, 0:1], cs_ref[:, 0:1]
    scores = []
    for b in range(pb):
        cum = cum_all[b * fh:(b + 1) * fh, :] + base
        base = cum[:, page - 1:page]
        kq = _dot((k_refs[b][...] * q).reshape(page * fh, fd).astype(BF16), ones)
        scores.append(jnp.sum(jnp.where(diag, kq.reshape(page, fh, V7X_LANES), 0.0), axis=0) - cum * LOG2E)
    m_new = m_run
    for s in scores:
        m_new = jnp.maximum(m_new, jnp.max(s, axis=1, keepdims=True))
    alpha = jnp.exp2(m_run - m_new)
    l_run = alpha * l_ref[:, 0:1]
    acc = acc_ref[...] * alpha
    for b in range(pb):
        p = jnp.exp2(scores[b] - m_new)
        l_run = l_run + jnp.sum(p, axis=1, keepdims=True)
        p_rows = jnp.where(diag, p, 0.0).reshape(page * fh, V7X_LANES).astype(BF16)
        p_rep = _dot(p_rows, jnp.ones((V7X_LANES, fd), BF16)).reshape(page, fh, fd)
        acc = acc + jnp.sum(p_rep * v_refs[b][...], axis=0)
    m_run = m_new
    m_ref[...] = jnp.broadcast_to(m_run, m_ref.shape)
    l_ref[...] = jnp.broadcast_to(l_run, l_ref.shape)
    acc_ref[...] = acc
    cs_ref[...] = jnp.broadcast_to(base, cs_ref.shape)

    @pl.when(j == pl.num_programs(1) - 1)
    def _():
        m_old, l_old = m_ref[:, 0:1], l_ref[:, 0:1]
        s_new = (jnp.sum(knew_ref[...] * q, axis=1, keepdims=True)
                 - (cs_ref[:, 0:1] + lfnew_ref[:, 0:1]) * LOG2E)
        m_fin = jnp.maximum(m_old, s_new)
        a_fin = jnp.exp2(m_old - m_fin)
        p_new = jnp.exp2(s_new - m_fin)
        out = (acc_ref[...] * a_fin + p_new * vnew_ref[...]) / (a_fin * l_old + p_new)
        o_ref[...] = out * _silu(z_ref[...])


def _decode(page_table, q3, cache_k, cache_v, cache_lft, knew, vnew, lfnew, zs, dm, layer):
    n, fh, fd, page = dm.dec_batch, dm.fh, dm.fd, dm.page
    pb = dm.pages_per_step
    per_seq = lambda a: pl.BlockSpec((None,) + a.shape[1:], lambda i, j, pt: (i,) + (0,) * (a.ndim - 1))
    kv_spec = lambda b: pl.BlockSpec((None, None, page, fh, fd),
                                     lambda i, j, pt: (layer, pt[i, j * pb + b], 0, 0, 0))
    lf_spec = lambda b: pl.BlockSpec((None, None, fh, page), lambda i, j, pt: (layer, pt[i, j * pb + b], 0, 0))
    grid_spec = pltpu.PrefetchScalarGridSpec(
        num_scalar_prefetch=1,
        grid=(n, dm.n_pages // pb),
        in_specs=([per_seq(q3), per_seq(knew), per_seq(vnew), per_seq(lfnew), per_seq(zs)]
                  + [kv_spec(b) for b in range(pb)] + [kv_spec(b) for b in range(pb)]
                  + [lf_spec(b) for b in range(pb)]),
        out_specs=pl.BlockSpec((None, fh, fd), lambda i, j, pt: (i, 0, 0)),
        scratch_shapes=[pltpu.VMEM((fh, V7X_LANES), F32), pltpu.VMEM((fh, V7X_LANES), F32),
                        pltpu.VMEM((fh, fd), F32), pltpu.VMEM((fh, V7X_LANES), F32)],
    )
    return pl.pallas_call(
        functools.partial(_decode_kernel, pages_per_step=pb),
        out_shape=jax.ShapeDtypeStruct((n, fh, fd), F32),
        grid_spec=grid_spec,
        compiler_params=_cparams(2),
        name="fox_sample",
    )(page_table, q3, knew, vnew, lfnew, zs, *([cache_k] * pb), *([cache_v] * pb), *([cache_lft] * pb))


def _merge_kernel(ya_ref, yb_ref, yc_ref, ga_ref, gb_ref, gc_ref, x_ref, wpa_ref, wpb_ref, wpc_ref,
                  wout_ref, lng_ref, lnb_ref, xo_ref, xb_ref, *, alpha):
    def branch(y_ref, w_ref, g_ref):
        return _sigmoid(g_ref[...].astype(F32)) * _dot(y_ref[...].astype(BF16), w_ref[...])

    merged = branch(ya_ref, wpa_ref, ga_ref) + branch(yb_ref, wpb_ref, gb_ref) + branch(yc_ref, wpc_ref, gc_ref)
    out = _dot(merged.astype(BF16), wout_ref[...])
    r = alpha * x_ref[...] + out
    mu = jnp.mean(r, axis=-1, keepdims=True)
    var = jnp.mean(jnp.square(r - mu), axis=-1, keepdims=True)
    xn = (r - mu) * lax.rsqrt(var + LN_EPS) * lng_ref[...] + lnb_ref[...]
    xo_ref[...] = xn
    xb_ref[...] = xn.astype(BF16)


def _merge(ya, yb, yc, pm, x, wts, dm, layer, tm, alpha):
    m, d = x.shape
    sw, mw, fw = dm.sw, dm.mw, dm.fw
    row = lambda i: (i, 0)
    lay3 = lambda i: (layer, 0, 0)
    g0 = dm.off['gate'] // d
    return pl.pallas_call(
        functools.partial(_merge_kernel, alpha=alpha),
        out_shape=(jax.ShapeDtypeStruct((m, d), F32), jax.ShapeDtypeStruct((m, d), BF16)),
        grid=(m // tm,),
        in_specs=[pl.BlockSpec((tm, sw), row), pl.BlockSpec((tm, mw), row), pl.BlockSpec((tm, fw), row),
                  pl.BlockSpec((tm, d), lambda i: (i, g0)), pl.BlockSpec((tm, d), lambda i: (i, g0 + 1)),
                  pl.BlockSpec((tm, d), lambda i: (i, g0 + 2)),
                  pl.BlockSpec((tm, d), row),
                  pl.BlockSpec((None, sw, d), lay3), pl.BlockSpec((None, mw, d), lay3),
                  pl.BlockSpec((None, fw, d), lay3), pl.BlockSpec((None, d, d), lay3),
                  pl.BlockSpec((None, 1, d), lay3), pl.BlockSpec((None, 1, d), lay3)],
        out_specs=(pl.BlockSpec((tm, d), row), pl.BlockSpec((tm, d), row)),
        compiler_params=_cparams(1),
        name="merge",
    )(ya, yb, yc, pm, pm, pm, x, *wts)


def _prep_in_weights(w_in, b_in, dm):
    scales = {'m_k': dm.md ** -0.5, 'c_q': dm.fd ** -0.5 * LOG2E}
    plan_main = tuple(dm.src[nm] + (dm.off[nm], scales.get(nm, 1.0)) for nm in dm.main_order)
    plan_kv = (dm.src['c_k'] + (0, 1.0), dm.src['c_v'] + (dm.fw, 1.0))
    w_main, w_kv = _repack(w_in, dm, plan_main, plan_kv)
    bias = lambda plan: jnp.concatenate([b_in[:, a:b] * sc for a, b, _, sc in plan], axis=1)[:, None, :]
    small = [dm.src[nm] for nm in ('m_i', 'm_f', 'c_f')]
    pad = V7X_LANES - sum(b - a for a, b in small)
    w_small = jnp.concatenate([w_in[:, :, a:b] for a, b in small]
                              + [jnp.zeros(w_in.shape[:2] + (pad,), F32)], axis=2).astype(BF16)
    b_small = jnp.concatenate([b_in[:, a:b] for a, b in small]
                              + [jnp.zeros((w_in.shape[0], pad), F32)], axis=1)[:, None, :]
    return w_main, bias(plan_main), w_kv, bias(plan_kv), w_small, b_small


def _s5_mats(b_re, b_im, c_re, c_im, dm):
    g = dm.g
    eye = jnp.eye(g, dtype=F32)
    bm = lambda b: jnp.einsum('gpc,gh->gchp', b, eye).reshape(dm.sw, dm.gp)
    cm = lambda c: jnp.einsum('gcp,gh->gphc', c, eye).reshape(dm.gp, dm.sw)
    bmat = jnp.concatenate([bm(b_re), bm(b_im)], axis=1).astype(BF16)
    cmat = jnp.concatenate([cm(c_re), -cm(c_im)], axis=0).astype(BF16)
    return bmat, cmat


def kernel(x_prompt, x_sample, cache_k, cache_v, cache_logf, page_table, state_ssm_re, state_ssm_im,
           state_mlstm_c, state_mlstm_n, state_mlstm_m, w_in, b_in, ssm_a_re, ssm_a_im, ssm_b_re,
           ssm_b_im, ssm_c_re, ssm_c_im, ssm_d, ssm_log_dt, w_glu, b_glu, mlstm_norm_g, w_pa, w_pb,
           w_pc, w_out, ln_g, ln_b):
    dm = _Dims(x_prompt, x_sample, cache_k, page_table, state_mlstm_c, w_in, ssm_a_re, ssm_b_re)
    depth, d = dm.depth, dm.d
    nb, nd, seq = dm.batch, dm.dec_batch, dm.seq
    alpha = (2 * depth) ** 0.25

    w_main, b_main, w_kv, b_kv, w_small, b_small = _prep_in_weights(w_in, b_in, dm)
    merge_w = (w_pa.astype(BF16), w_pb.astype(BF16), w_pc.astype(BF16), w_out.astype(BF16),
               ln_g[:, None, :], ln_b[:, None, :])
    wglu_b = w_glu.astype(BF16)
    bglu3, d3, normg3 = b_glu[:, None, :], ssm_d[:, None, :], mlstm_norm_g[:, None, :]
    clft = jnp.swapaxes(cache_logf, 2, 3)
    page_table = page_table.astype(jnp.int32)
    heads = lambda a: a.reshape(nd, dm.fh, dm.fd)

    xp = x_prompt.reshape(nb * seq, d)
    xs = x_sample.reshape(nd, d)
    xpb, xsb = xp.astype(BF16), xs.astype(BF16)
    cf = slice(dm.lane_cf, dm.lane_cf + dm.fh)
    outs = {k: [] for k in ('kp', 'vp', 'lfp', 'ks', 'vs', 'lfs', 'srp', 'sip', 'srs', 'sis',
                            'mcp', 'mnp', 'mmp', 'mcs', 'mns', 'mms')}
    for l in range(depth):
        disc = _s5_disc(ssm_a_re[l], ssm_a_im[l], ssm_log_dt[l]).reshape(4, dm.gp)
        disc = jnp.concatenate([disc, jnp.zeros((V7X_SUBLANES - 4, dm.gp), F32)], axis=0)
        bmat, cmat = _s5_mats(ssm_b_re[l], ssm_b_im[l], ssm_c_re[l], ssm_c_im[l], dm)

        pm = _inproj_main(xpb, w_main, b_main, l, dm.tm_in, dm.tn_main, BF16)
        k32, v32, s32, kb, vb = _inproj_kv(xpb, w_kv, b_kv, w_small, b_small, l, dm.tm_in, dm.fw)
        lf, bc, st, bct, qx, kx = _gates(s32, nb, seq, dm.chunk, dm.fh, dm.lane_cf)
        ya, sst = _s5_prompt(pm, dm, l, bmat, cmat, disc, d3, wglu_b, bglu3)
        yb, caug, mo = _mlstm_prompt(pm, bc, st, bct, normg3, dm, l)
        yc = _flash(pm, kb, vb, qx, kx, dm)
        xp, xpb = _merge(ya, yb, yc, pm, xp, merge_w, dm, l, dm.tm_merge, alpha)
        outs['kp'].append(k32)
        outs['vp'].append(v32)
        outs['lfp'].append(lf[:, cf])
        outs['srp'].append(sst[:, 0])
        outs['sip'].append(sst[:, 1])
        outs['mcp'].append(caug[..., :dm.md])
        outs['mnp'].append(caug[..., dm.md])
        outs['mmp'].append(mo[:, :dm.mh, 0])

        ps = _inproj_main(xsb, w_main, b_main, l, nd, dm.tn_main, F32)
        k32s, v32s, s32s, _, _ = _inproj_kv(xsb, w_kv, b_kv, w_small, b_small, l, nd, dm.fw)
        lfs = _logsig(s32s)
        yas, x1r, x1i = _s5_sample(ps, dm, l, state_ssm_re[l].reshape(nd, dm.gp),
                                   state_ssm_im[l].reshape(nd, dm.gp), bmat, cmat, disc, d3, wglu_b, bglu3)
        ybs, c1, n1, m1 = _mlstm_sample(ps, s32s, state_mlstm_c, state_mlstm_n, state_mlstm_m, normg3, dm, l)
        qs = ps[:, dm.off['c_q']:dm.off['c_q'] + dm.fw]
        zs = ps[:, dm.off['c_z']:dm.off['c_z'] + dm.fw]
        lfnew = jnp.broadcast_to(lfs[:, cf][:, :, None], (nd, dm.fh, V7X_LANES))
        ycs = _decode(page_table, heads(qs), cache_k, cache_v, clft, heads(k32s), heads(v32s), lfnew,
                      heads(zs), dm, l).reshape(nd, dm.fw)
        xs, xsb = _merge(yas, ybs, ycs, ps, xs, merge_w, dm, l, nd, alpha)
        outs['ks'].append(k32s)
        outs['vs'].append(v32s)
        outs['lfs'].append(lfs[:, cf])
        outs['srs'].append(x1r)
        outs['sis'].append(x1i)
        outs['mcs'].append(c1)
        outs['mns'].append(n1)
        outs['mms'].append(m1)

    stk = lambda k, shape: jnp.stack(outs[k]).reshape((depth,) + shape)
    return (xp.reshape(nb, seq, d), xs.reshape(nd, 1, d),
            stk('kp', (nb, seq, dm.fh, dm.fd)), stk('vp', (nb, seq, dm.fh, dm.fd)), stk('lfp', (nb, seq, dm.fh)),
            stk('ks', (nd, 1, dm.fh, dm.fd)), stk('vs', (nd, 1, dm.fh, dm.fd)), stk('lfs', (nd, 1, dm.fh)),
            stk('srp', (nb, dm.g, dm.p)), stk('sip', (nb, dm.g, dm.p)),
            stk('srs', (nd, dm.g, dm.p)), stk('sis', (nd, dm.g, dm.p)),
            stk('mcp', (nb, dm.mh, dm.md, dm.md)), stk('mnp', (nb, dm.mh, dm.md)), stk('mmp', (nb, dm.mh)),
            stk('mcs', (nd, dm.mh, dm.md, dm.md)), stk('mns', (nd, dm.mh, dm.md)), stk('mms', (nd, dm.mh)))
```

```python
import functools
import math

import jax
import jax.numpy as jnp
from jax import lax
from jax.experimental import pallas as pl
from jax.experimental.pallas import tpu as pltpu

F32 = jnp.float32
BF16 = jnp.bfloat16

V7X_LANES = 128
V7X_SUBLANES = 8
V7X_VMEM_LIMIT_BYTES = 56 * 1024 * 1024

LN_EPS = 1e-5
NEG_INF = float("-inf")
LOG2E = math.log2(math.e)
V7X_MXU_DIM = 256


def _cparams(n_axes, flags=None):
    return pltpu.CompilerParams(dimension_semantics=("arbitrary",) * n_axes,
                                vmem_limit_bytes=V7X_VMEM_LIMIT_BYTES, flags=flags)


def _sigmoid(x):
    return 1.0 / (1.0 + jnp.exp(-x))


def _silu(x):
    return x * _sigmoid(x)


def _log_sigmoid(x):
    return jnp.minimum(x, 0.0) - jnp.log1p(jnp.exp(-jnp.abs(x)))


def _gelu_tanh(x):
    c = math.sqrt(2.0 / math.pi)
    return 0.5 * x * (1.0 + jnp.tanh(c * (x + 0.044715 * (x * x * x))))


def _dot(a, b):
    return jnp.dot(a, b, preferred_element_type=F32)


def _dot_nt(a, b):
    return lax.dot_general(a, b, (((1,), (1,)), ((), ())), preferred_element_type=F32)


def _split3(x):
    hi = x.astype(BF16)
    r1 = x - hi.astype(F32)
    mid = r1.astype(BF16)
    lo = (r1 - mid.astype(F32)).astype(BF16)
    return hi, mid, lo


def _cumsum_rows(x):
    n = x.shape[0]
    r = lax.broadcasted_iota(jnp.int32, (n, n), 0)
    c = lax.broadcasted_iota(jnp.int32, (n, n), 1)
    tri = (r >= c).astype(BF16)
    hi, mid, lo = _split3(x)
    return _dot(tri, hi) + _dot(tri, mid) + _dot(tri, lo)


def _cumsum_lanes(x):
    n = x.shape[1]
    r = lax.broadcasted_iota(jnp.int32, (n, n), 0)
    c = lax.broadcasted_iota(jnp.int32, (n, n), 1)
    tri = (r <= c).astype(BF16)
    hi, mid, lo = _split3(x)
    return _dot(hi, tri) + _dot(mid, tri) + _dot(lo, tri)


class _Dims:
    def __init__(self, x_prompt, x_sample, cache_k, page_table, state_mlstm_c, w_in, ssm_a_re, ssm_b_re):
        self.batch, self.seq, self.d = x_prompt.shape
        self.dec_batch = x_sample.shape[0]
        assert x_sample.shape[1] == 1
        self.depth = w_in.shape[0]
        self.g, self.p = ssm_a_re.shape[1], ssm_a_re.shape[2]
        self.gs = ssm_b_re.shape[3]
        self.sw = self.g * self.gs
        self.gp = self.g * self.p
        self.mh, self.md = state_mlstm_c.shape[2], state_mlstm_c.shape[3]
        self.mw = self.mh * self.md
        self.page, self.fh, self.fd = cache_k.shape[2], cache_k.shape[3], cache_k.shape[4]
        self.fw = self.fh * self.fd
        self.n_pages = page_table.shape[1]
        self.n_pool = cache_k.shape[1]
        d, sw, mw, fw, mh, fh = self.d, self.sw, self.mw, self.fw, self.mh, self.fh
        widths = (('ssm_u', sw), ('ssm_z', sw), ('m_q', mw), ('m_k', mw), ('m_v', mw), ('m_i', mh),
                  ('m_f', mh), ('m_o', mw), ('m_z', mw), ('c_q', fw), ('c_k', fw), ('c_v', fw),
                  ('c_f', fh), ('c_z', fw), ('gate', 3 * d))
        self.src = {}
        off = 0
        for nm, wd in widths:
            self.src[nm] = (off, off + wd)
            off += wd
        assert off == w_in.shape[2]
        self.main_order = ('gate', 'ssm_u', 'ssm_z', 'm_q', 'm_k', 'm_v', 'm_o', 'm_z', 'c_q', 'c_z')
        self.off = {}
        off = 0
        for nm in self.main_order:
            self.off[nm] = off
            off += self.src[nm][1] - self.src[nm][0]
        self.n_main = off
        self.n_kv = 2 * fw + V7X_LANES
        self.lane_li, self.lane_lf, self.lane_cf = (self.src[nm][0] % V7X_LANES for nm in ('m_i', 'm_f', 'c_f'))
        assert self.lane_li + mh <= self.lane_lf and self.lane_lf + mh <= self.lane_cf
        assert self.lane_cf + fh <= V7X_LANES
        assert self.md % V7X_LANES == 0 and self.fd % V7X_LANES == 0 and sw % V7X_LANES == 0
        assert fh == V7X_SUBLANES and self.page == V7X_LANES
        for nm in ('ssm_u', 'ssm_z', 'm_q', 'm_k', 'm_v', 'm_o', 'm_z'):
            assert self.off[nm] % sw == 0 and self.off[nm] % mw == 0
        assert self.off['c_q'] % self.fd == 0 and self.off['c_z'] % self.fd == 0
        self.m_rows = self.batch * self.seq
        self.tm_in = min(512, self.m_rows)
        self.tn_main = max(t for t in range(V7X_LANES, 3072 + 1, V7X_LANES) if self.n_main % t == 0)
        self.chunk = min(256, self.seq)
        self.s5_chunk = min(512, self.seq)
        self.tq = min(512, self.seq)
        self.tk = self.tq
        assert self.tq % self.tk == 0
        self.tm_merge = min(256, self.m_rows)
        self.pages_per_step = max(p for p in (8, 4, 2, 1) if self.n_pages % p == 0)
        self.flash_heads = max(hp for hp in (2, 1) if fh % hp == 0 and self.off['c_q'] % (hp * self.fd) == 0
                               and self.off['c_z'] % (hp * self.fd) == 0)
        self.tr_repack = min(128, self.d)
        assert self.seq % self.chunk == 0 and self.seq % self.tq == 0 and self.seq % self.s5_chunk == 0
        assert self.m_rows % self.tm_in == 0 and self.m_rows % self.tm_merge == 0
        assert self.d % self.tr_repack == 0


def _mm_bias_kernel(x_ref, w_ref, b_ref, o_ref):
    acc = _dot(x_ref[...].astype(BF16), w_ref[...]) + b_ref[...]
    o_ref[...] = acc.astype(o_ref.dtype)


def _inproj_main(xb, w_all, b_all, layer, tm, tn, out_dtype):
    m, d = xb.shape
    n = w_all.shape[2]
    return pl.pallas_call(
        _mm_bias_kernel,
        out_shape=jax.ShapeDtypeStruct((m, n), out_dtype),
        grid=(n // tn, m // tm),
        in_specs=[pl.BlockSpec((tm, d), lambda j, i: (i, 0)),
                  pl.BlockSpec((None, d, tn), lambda j, i: (layer, 0, j)),
                  pl.BlockSpec((None, 1, tn), lambda j, i: (layer, 0, j))],
        out_specs=pl.BlockSpec((tm, tn), lambda j, i: (i, j)),
        compiler_params=_cparams(2),
        name="inproj_main",
    )(xb, w_all, b_all)


def _kv_kernel(x_ref, w_ref, b_ref, ws_ref, bs_ref, k32_ref, v32_ref, s32_ref, kb_ref, vb_ref, *, fw):
    x = x_ref[...].astype(BF16)
    acc = _dot(x, w_ref[...]) + b_ref[...]
    k = acc[:, :fw]
    v = acc[:, fw:]
    k32_ref[...] = k
    v32_ref[...] = v
    s32_ref[...] = _dot(x, ws_ref[...]) + bs_ref[...]
    kb_ref[...] = k.astype(BF16)
    vb_ref[...] = v.astype(BF16)


def _inproj_kv(xb, w_kv, b_kv, w_small, b_small, layer, tm, fw):
    m, d = xb.shape
    row = lambda i: (i, 0)
    lay3 = lambda i: (layer, 0, 0)
    return pl.pallas_call(
        functools.partial(_kv_kernel, fw=fw),
        out_shape=(jax.ShapeDtypeStruct((m, fw), F32), jax.ShapeDtypeStruct((m, fw), F32),
                   jax.ShapeDtypeStruct((m, V7X_LANES), F32),
                   jax.ShapeDtypeStruct((m, fw), BF16), jax.ShapeDtypeStruct((m, fw), BF16)),
        grid=(m // tm,),
        in_specs=[pl.BlockSpec((tm, d), row),
                  pl.BlockSpec((None, d, 2 * fw), lay3), pl.BlockSpec((None, 1, 2 * fw), lay3),
                  pl.BlockSpec((None, d, V7X_LANES), lay3), pl.BlockSpec((None, 1, V7X_LANES), lay3)],
        out_specs=(pl.BlockSpec((tm, fw), row), pl.BlockSpec((tm, fw), row),
                   pl.BlockSpec((tm, V7X_LANES), row),
                   pl.BlockSpec((tm, fw), row), pl.BlockSpec((tm, fw), row)),
        compiler_params=_cparams(1),
        name="inproj_kv",
    )(xb, w_kv, b_kv, w_small, b_small)


def _repack_kernel(w_ref, om_ref, okv_ref, os_ref, *, plan_main, plan_kv, plan_small):
    for o_ref, plan in ((om_ref, plan_main), (okv_ref, plan_kv)):
        for a, b, dst, scale in plan:
            w = w_ref[:, a:b]
            if scale != 1.0:
                w = w * scale
            o_ref[:, dst:dst + (b - a)] = w.astype(BF16)
    lane = lax.broadcasted_iota(jnp.int32, os_ref.shape, 1)
    small = jnp.zeros(os_ref.shape, F32)
    for a, b in plan_small:
        base = a - a % V7X_LANES
        small = jnp.where((lane >= a - base) & (lane < b - base), w_ref[:, base:base + V7X_LANES], small)
    os_ref[...] = small.astype(BF16)


def _repack(w_in, dm, plan_main, plan_kv, plan_small):
    depth, d, n_in = w_in.shape
    tr = dm.tr_repack
    blk = lambda n: pl.BlockSpec((None, tr, n), lambda l, r: (l, r, 0))
    return pl.pallas_call(
        functools.partial(_repack_kernel, plan_main=plan_main, plan_kv=plan_kv, plan_small=plan_small),
        out_shape=(jax.ShapeDtypeStruct((depth, d, dm.n_main), BF16),
                   jax.ShapeDtypeStruct((depth, d, 2 * dm.fw), BF16),
                   jax.ShapeDtypeStruct((depth, d, V7X_LANES), BF16)),
        grid=(depth, d // tr),
        in_specs=[blk(n_in)],
        out_specs=(blk(dm.n_main), blk(2 * dm.fw), blk(V7X_LANES)),
        compiler_params=_cparams(2),
        name="repack_w_in",
    )(w_in)


def _gates_kernel(s_ref, lf_ref, bc_ref, st_ref, bct_ref, qx_ref, kx_ref, carry_ref, *, fh, lane_cf):
    @pl.when(pl.program_id(1) == 0)
    def _():
        carry_ref[...] = jnp.zeros_like(carry_ref)

    s = s_ref[...]
    lf = _log_sigmoid(s)
    bc = _cumsum_rows(lf)
    fc = bc + carry_ref[0:1, :]
    carry_ref[...] = jnp.broadcast_to(fc[-1:, :], carry_ref.shape)
    lf_ref[...] = lf
    bc_ref[...] = bc
    st_ref[...] = s.T
    bct_ref[...] = bc.T
    lane = lax.broadcasted_iota(jnp.int32, s.shape, 1)
    for h in range(fh):
        hi, mid, lo = (t.astype(F32) for t in _split3(fc[:, lane_cf + h:lane_cf + h + 1] * LOG2E))
        qx = jnp.where(lane == 0, hi, jnp.where(lane == 1, mid, jnp.where(lane == 2, lo,
                       jnp.where(lane < 6, 1.0, 0.0))))
        kx = jnp.where(lane < 3, 1.0, jnp.where(lane == 3, -hi, jnp.where(lane == 4, -mid,
                       jnp.where(lane == 5, -lo, 0.0))))
        hs = slice(h * V7X_LANES, (h + 1) * V7X_LANES)
        qx_ref[:, hs] = qx.astype(BF16)
        kx_ref[:, hs] = kx.astype(BF16)


def _gates(s32, n_seq, seq, chunk, fh, lane_cf):
    m = s32.shape[0]
    nc = seq // chunk
    row = pl.BlockSpec((chunk, V7X_LANES), lambda n, c: (n * nc + c, 0))
    tr = pl.BlockSpec((None, V7X_LANES, chunk), lambda n, c: (n, 0, c))
    xrow = pl.BlockSpec((chunk, fh * V7X_LANES), lambda n, c: (n * nc + c, 0))
    rs = jax.ShapeDtypeStruct((m, V7X_LANES), F32)
    ts = jax.ShapeDtypeStruct((n_seq, V7X_LANES, seq), F32)
    xs = jax.ShapeDtypeStruct((m, fh * V7X_LANES), BF16)
    return pl.pallas_call(
        functools.partial(_gates_kernel, fh=fh, lane_cf=lane_cf),
        out_shape=(rs, rs, ts, ts, xs, xs),
        grid=(n_seq, nc),
        in_specs=[row],
        out_specs=(row, row, tr, tr, xrow, xrow),
        scratch_shapes=[pltpu.VMEM((V7X_SUBLANES, V7X_LANES), F32)],
        compiler_params=_cparams(2),
        name="gates",
    )(s32)


def _logsig_kernel(s_ref, o_ref):
    o_ref[...] = _log_sigmoid(s_ref[...])


def _logsig(s32):
    return pl.pallas_call(_logsig_kernel, out_shape=jax.ShapeDtypeStruct(s32.shape, F32), name="logsig")(s32)


def _s5_disc_kernel(are_ref, aim_ref, ldt_ref, o_ref):
    a_re, a_im = are_ref[...], aim_ref[...]
    dt = jnp.exp(ldt_ref[...])
    mag = jnp.exp(a_re * dt)
    ab_re, ab_im = mag * jnp.cos(a_im * dt), mag * jnp.sin(a_im * dt)
    nr, ni = ab_re - 1.0, ab_im
    den = a_re * a_re + a_im * a_im
    o_ref[0] = ab_re
    o_ref[1] = ab_im
    o_ref[2] = (nr * a_re + ni * a_im) / den
    o_ref[3] = (ni * a_re - nr * a_im) / den


def _s5_disc(a_re, a_im, log_dt):
    ldt = jnp.broadcast_to(log_dt[..., None], a_re.shape)
    return pl.pallas_call(_s5_disc_kernel, out_shape=jax.ShapeDtypeStruct((4,) + a_re.shape, F32),
                          name="s5_disc")(a_re, a_im, ldt)


def _s5_in_map(u, bmat_ref, gp, sw):
    kt = min(V7X_MXU_DIM, sw)
    ct = gp * kt // sw
    re, im = [], []
    for i in range(sw // kt):
        uk = u[:, i * kt:(i + 1) * kt]
        re.append(_dot(uk, bmat_ref[i * kt:(i + 1) * kt, i * ct:(i + 1) * ct]))
        im.append(_dot(uk, bmat_ref[i * kt:(i + 1) * kt, gp + i * ct:gp + (i + 1) * ct]))
    return jnp.concatenate(re, axis=1), jnp.concatenate(im, axis=1)


def _s5_out_map(xr, xi, cmat_ref, gp, sw):
    nt = min(V7X_MXU_DIM, sw)
    ct = gp * nt // sw
    ys = []
    for i in range(sw // nt):
        cs, ns = slice(i * ct, (i + 1) * ct), slice(i * nt, (i + 1) * nt)
        ys.append(_dot(xr[:, cs], cmat_ref[i * ct:(i + 1) * ct, ns])
                  + _dot(xi[:, cs], cmat_ref[gp + i * ct:gp + (i + 1) * ct, ns]))
    return jnp.concatenate(ys, axis=1)


def _s5_tail(y_state, u, z, dvec, wglu, bglu):
    gact = _gelu_tanh(y_state + dvec * u)
    glu = _dot(gact.astype(BF16), wglu) + bglu
    return gact * _sigmoid(glu) * _silu(z)


def _s5_prompt_kernel(u_ref, z_ref, bmat_ref, cmat_ref, disc_ref, d_ref, wglu_ref, bglu_ref,
                      ya_ref, st_ref, buf_ref, carry_ref, *, gp, sw):
    c = pl.program_id(1)

    @pl.when(c == 0)
    def _():
        carry_ref[...] = jnp.zeros_like(carry_ref)

    u = u_ref[...]
    br, bi = _s5_in_map(u, bmat_ref, gp, sw)
    ar, ai = disc_ref[0:1, :], disc_ref[1:2, :]
    kr, ki = disc_ref[2:3, :], disc_ref[3:4, :]
    buf_ref[:, :gp] = kr * br - ki * bi
    buf_ref[:, gp:] = kr * bi + ki * br

    def step(t, carry):
        xr, xi = carry
        nr = ar * xr - ai * xi + buf_ref[pl.ds(t, 1), :gp]
        ni = ar * xi + ai * xr + buf_ref[pl.ds(t, 1), gp:]
        buf_ref[pl.ds(t, 1), :gp] = nr
        buf_ref[pl.ds(t, 1), gp:] = ni
        return nr, ni

    xr, xi = lax.fori_loop(0, u.shape[0], step, (carry_ref[0:1, :], carry_ref[1:2, :]))
    carry_ref[0:1, :] = xr
    carry_ref[1:2, :] = xi
    y_state = _s5_out_map(buf_ref[:, :gp].astype(BF16), buf_ref[:, gp:].astype(BF16), cmat_ref, gp, sw)
    ya = _s5_tail(y_state, u.astype(F32), z_ref[...].astype(F32), d_ref[...], wglu_ref[...], bglu_ref[...])
    ya_ref[...] = ya.astype(ya_ref.dtype)

    @pl.when(c == pl.num_programs(1) - 1)
    def _():
        st_ref[...] = carry_ref[...]


def _s5_prompt(pm, dm, layer, bmat, cmat, disc, dvec, wglu, bglu):
    tc = dm.s5_chunk
    nc = dm.seq // tc
    sw, gp = dm.sw, dm.gp
    cu, cz = dm.off['ssm_u'] // sw, dm.off['ssm_z'] // sw
    const2 = lambda n, c: (0, 0)
    lay3 = lambda n, c: (layer, 0, 0)
    return pl.pallas_call(
        functools.partial(_s5_prompt_kernel, gp=gp, sw=sw),
        out_shape=(jax.ShapeDtypeStruct((dm.m_rows, sw), BF16),
                   jax.ShapeDtypeStruct((dm.batch, V7X_SUBLANES, gp), F32)),
        grid=(dm.batch, nc),
        in_specs=[pl.BlockSpec((tc, sw), lambda n, c: (n * nc + c, cu)),
                  pl.BlockSpec((tc, sw), lambda n, c: (n * nc + c, cz)),
                  pl.BlockSpec((None,) + bmat.shape[1:], lay3), pl.BlockSpec((None,) + cmat.shape[1:], lay3),
                  pl.BlockSpec((None,) + disc.shape[1:], lay3),
                  pl.BlockSpec((None, 1, sw), lay3), pl.BlockSpec((None, sw, sw), lay3),
                  pl.BlockSpec((None, 1, sw), lay3)],
        out_specs=(pl.BlockSpec((tc, sw), lambda n, c: (n * nc + c, 0)),
                   pl.BlockSpec((None, V7X_SUBLANES, gp), lambda n, c: (n, 0, 0))),
        scratch_shapes=[pltpu.VMEM((tc, 2 * gp), F32), pltpu.VMEM((V7X_SUBLANES, gp), F32)],
        compiler_params=_cparams(2),
        name="s5_prompt",
    )(pm, pm, bmat, cmat, disc, dvec, wglu, bglu)


def _s5_sample_kernel(u_ref, z_ref, x0r_ref, x0i_ref, bmat_ref, cmat_ref, disc_ref, d_ref, wglu_ref,
                      bglu_ref, ya_ref, x1r_ref, x1i_ref, *, gp, sw):
    u = u_ref[...]
    br, bi = _s5_in_map(u.astype(BF16), bmat_ref, gp, sw)
    ar, ai = disc_ref[0:1, :], disc_ref[1:2, :]
    kr, ki = disc_ref[2:3, :], disc_ref[3:4, :]
    x0r, x0i = x0r_ref[...], x0i_ref[...]
    x1r = (kr * br - ki * bi) + (ar * x0r - ai * x0i)
    x1i = (kr * bi + ki * br) + (ar * x0i + ai * x0r)
    x1r_ref[...] = x1r
    x1i_ref[...] = x1i
    y_state = _s5_out_map(x1r.astype(BF16), x1i.astype(BF16), cmat_ref, gp, sw)
    ya_ref[...] = _s5_tail(y_state, u, z_ref[...], d_ref[0], wglu_ref[0], bglu_ref[0])


def _s5_sample(ps, dm, layer, x0r, x0i, bmat, cmat, disc, dvec, wglu, bglu):
    n, sw, gp = dm.dec_batch, dm.sw, dm.gp
    cu, cz = dm.off['ssm_u'] // sw, dm.off['ssm_z'] // sw
    full = lambda a: pl.BlockSpec(a.shape, lambda i: (0,) * a.ndim)
    lay3 = lambda i: (layer, 0, 0)
    return pl.pallas_call(
        functools.partial(_s5_sample_kernel, gp=gp, sw=sw),
        out_shape=(jax.ShapeDtypeStruct((n, sw), F32), jax.ShapeDtypeStruct((n, gp), F32),
                   jax.ShapeDtypeStruct((n, gp), F32)),
        grid=(1,),
        in_specs=[pl.BlockSpec((n, sw), lambda i: (0, cu)), pl.BlockSpec((n, sw), lambda i: (0, cz)),
                  full(x0r), full(x0i),
                  pl.BlockSpec((None,) + bmat.shape[1:], lay3), pl.BlockSpec((None,) + cmat.shape[1:], lay3),
                  pl.BlockSpec((None,) + disc.shape[1:], lay3),
                  pl.BlockSpec((1, 1, sw), lay3), pl.BlockSpec((1, sw, sw), lay3),
                  pl.BlockSpec((1, 1, sw), lay3)],
        out_specs=(pl.BlockSpec((n, sw), lambda i: (0, 0)), pl.BlockSpec((n, gp), lambda i: (0, 0)),
                   pl.BlockSpec((n, gp), lambda i: (0, 0))),
        compiler_params=_cparams(1),
        name="s5_sample",
    )(ps, ps, x0r, x0i, bmat, cmat, disc, dvec, wglu, bglu)


def _head_norm(h):
    mu = jnp.mean(h, axis=-1, keepdims=True)
    var = jnp.mean(jnp.square(h - mu), axis=-1, keepdims=True)
    return (h - mu) * lax.rsqrt(var + LN_EPS)


def _mlstm_prompt_kernel(q_ref, k_ref, v_ref, o_ref, z_ref, bc_ref, st_ref, bct_ref, g_ref,
                         yb_ref, cout_ref, mout_ref, cst_ref, mst_ref, *, mh, md, lane_li, lane_lf):
    c = pl.program_id(1)

    @pl.when(c == 0)
    def _():
        cst_ref[...] = jnp.zeros_like(cst_ref)
        mst_ref[...] = jnp.zeros_like(mst_ref)

    ln = q_ref.shape[0]
    row = lax.broadcasted_iota(jnp.int32, (ln, ln), 0)
    col = lax.broadcasted_iota(jnp.int32, (ln, ln), 1)
    causal = row >= col
    ones_col = (lax.broadcasted_iota(jnp.int32, (ln, V7X_LANES), 1) == 0).astype(BF16)
    for h in range(mh):
        hs = slice(h * md, (h + 1) * md)
        q, k, v = q_ref[:, hs], k_ref[:, hs], v_ref[:, hs]
        a_row = st_ref[lane_li + h:lane_li + h + 1, :] - bct_ref[lane_lf + h:lane_lf + h + 1, :]
        b_col = bc_ref[:, lane_lf + h:lane_lf + h + 1]
        m0 = mst_ref[h:h + 1, 0:1]
        amat = jnp.where(causal, a_row, NEG_INF)
        m_col = jnp.maximum(m0, jnp.max(amat, axis=1, keepdims=True))
        w = jnp.exp(amat - m_col)
        s_inter = jnp.exp(m0 - m_col)
        qk = (_dot_nt(q, k) * w).astype(BF16)
        v_aug = jnp.concatenate([v, ones_col], axis=1)
        c0 = cst_ref[h]
        num_aug = s_inter * _dot(q, c0.astype(BF16)) + _dot(qk, v_aug)
        num, den = num_aug[:, :md], num_aug[:, md:md + 1]
        hh = num / jnp.maximum(jnp.abs(den), jnp.exp(-(b_col + m_col)))
        yb = (_head_norm(hh) * g_ref[:, hs] * _sigmoid(o_ref[:, hs].astype(F32))
              * _silu(z_ref[:, hs].astype(F32)))
        yb_ref[:, hs] = yb.astype(yb_ref.dtype)
        m_end = m_col[ln - 1:ln, :]
        decay = jnp.exp(m0 - m_end)
        ws_row = jnp.exp(a_row - m_end)
        kw = (k.astype(F32).T * ws_row).astype(BF16)
        cst_ref[h] = decay * c0 + _dot(kw, v_aug)
        mst_ref[h:h + 1, :] = jnp.broadcast_to(b_col[ln - 1:ln, :] + m_end, (1, V7X_LANES))

    @pl.when(c == pl.num_programs(1) - 1)
    def _():
        cout_ref[...] = cst_ref[...]
        mout_ref[...] = mst_ref[...]


def _mlstm_prompt(pm, bc, st, bct, norm_g, dm, layer):
    ln = dm.chunk
    nc = dm.seq // ln
    mw, mh, md = dm.mw, dm.mh, dm.md
    aug = md + V7X_LANES
    rowblk = lambda nm: pl.BlockSpec((ln, mw), lambda n, c, o=dm.off[nm] // mw: (n * nc + c, o))
    tr = pl.BlockSpec((None, V7X_LANES, ln), lambda n, c: (n, 0, c))
    return pl.pallas_call(
        functools.partial(_mlstm_prompt_kernel, mh=mh, md=md, lane_li=dm.lane_li, lane_lf=dm.lane_lf),
        out_shape=(jax.ShapeDtypeStruct((dm.m_rows, mw), BF16),
                   jax.ShapeDtypeStruct((dm.batch, mh, md, aug), F32),
                   jax.ShapeDtypeStruct((dm.batch, V7X_SUBLANES, V7X_LANES), F32)),
        grid=(dm.batch, nc),
        in_specs=[rowblk('m_q'), rowblk('m_k'), rowblk('m_v'), rowblk('m_o'), rowblk('m_z'),
                  pl.BlockSpec((ln, V7X_LANES), lambda n, c: (n * nc + c, 0)), tr, tr,
                  pl.BlockSpec((None, 1, mw), lambda n, c: (layer, 0, 0))],
        out_specs=(pl.BlockSpec((ln, mw), lambda n, c: (n * nc + c, 0)),
                   pl.BlockSpec((None, mh, md, aug), lambda n, c: (n, 0, 0, 0)),
                   pl.BlockSpec((None, V7X_SUBLANES, V7X_LANES), lambda n, c: (n, 0, 0))),
        scratch_shapes=[pltpu.VMEM((mh, md, aug), F32), pltpu.VMEM((V7X_SUBLANES, V7X_LANES), F32)],
        compiler_params=_cparams(2),
        name="mlstm_prompt",
    )(pm, pm, pm, pm, pm, bc, st, bct, norm_g)


def _mlstm_sample_kernel(ps_ref, s_ref, c0_ref, n0_ref, m0_ref, g_ref, yb_ref, c1_ref, n1_ref, m1_ref,
                         *, nseq, mh, md, offs, lane_li, lane_lf):
    s = s_ref[...]
    lfa = _log_sigmoid(s)
    eye = (lax.broadcasted_iota(jnp.int32, (md, md), 0) == lax.broadcasted_iota(jnp.int32, (md, md), 1))
    for n in range(nseq):
        for h in range(mh):
            def seg(nm):
                o = offs[nm] + h * md
                return ps_ref[n:n + 1, o:o + md]
            q = seg('m_q').astype(BF16)
            k = seg('m_k').astype(BF16)
            v = seg('m_v').astype(BF16).astype(F32)
            li = s[n:n + 1, lane_li + h:lane_li + h + 1]
            lf = lfa[n:n + 1, lane_lf + h:lane_lf + h + 1]
            m0 = m0_ref[n:n + 1, h:h + 1]
            c0 = c0_ref[n, h]
            n0 = n0_ref[n, h:h + 1, :]
            inter = lf + m0
            m1 = jnp.maximum(inter, li)
            w = jnp.exp(li - m1)
            s_inter = jnp.exp(inter - m1)
            kf = k.astype(F32)
            qk = jnp.sum(q.astype(F32) * kf, axis=1, keepdims=True) * w
            qc = _dot(jnp.broadcast_to(q, (V7X_SUBLANES, md)), c0.astype(BF16))[0:1, :]
            num = s_inter * qc + qk * v
            den = s_inter * jnp.sum(q.astype(F32) * n0, axis=1, keepdims=True) + qk
            hh = num / jnp.maximum(jnp.abs(den), jnp.exp(-m1))
            go = g_ref[0, :, h * md:(h + 1) * md]
            yb = _head_norm(hh) * go * _sigmoid(seg('m_o')) * _silu(seg('m_z'))
            yb_ref[n:n + 1, h * md:(h + 1) * md] = yb
            k_col = jnp.sum(jnp.where(eye, kf, 0.0), axis=1, keepdims=True)
            c1_ref[n, h] = s_inter * c0 + (w * k_col) * v
            n1_ref[n, h:h + 1, :] = s_inter * n0 + w * kf
            m1_ref[n:n + 1, h:h + 1] = m1


def _mlstm_sample(ps, s32, c0, n0, m0, norm_g, dm, layer):
    n, mh, md, mw = dm.dec_batch, dm.mh, dm.md, dm.mw
    offs = {nm: dm.off[nm] for nm in ('m_q', 'm_k', 'm_v', 'm_o', 'm_z')}
    full = lambda a: pl.BlockSpec(a.shape, lambda i: (0,) * a.ndim)
    st4 = lambda i: (layer, 0, 0, 0, 0)
    return pl.pallas_call(
        functools.partial(_mlstm_sample_kernel, nseq=n, mh=mh, md=md, offs=offs, lane_li=dm.lane_li,
                          lane_lf=dm.lane_lf),
        out_shape=(jax.ShapeDtypeStruct((n, mw), F32), jax.ShapeDtypeStruct((n, mh, md, md), F32),
                   jax.ShapeDtypeStruct((n, mh, md), F32), jax.ShapeDtypeStruct((n, mh), F32)),
        grid=(1,),
        in_specs=[full(ps), full(s32),
                  pl.BlockSpec((None, n, mh, md, md), st4),
                  pl.BlockSpec((None, n, mh, md), lambda i: (layer, 0, 0, 0)),
                  pl.BlockSpec((None, n, mh), lambda i: (layer, 0, 0)),
                  pl.BlockSpec((1, 1, mw), lambda i: (layer, 0, 0))],
        out_specs=(pl.BlockSpec((n, mw), lambda i: (0, 0)),
                   pl.BlockSpec((n, mh, md, md), lambda i: (0, 0, 0, 0)),
                   pl.BlockSpec((n, mh, md), lambda i: (0, 0, 0)),
                   pl.BlockSpec((n, mh), lambda i: (0, 0))),
        compiler_params=_cparams(1),
        name="mlstm_sample",
    )(ps, s32, c0, n0, m0, norm_g)


def _flash_kernel(q_ref, qx_ref, z_ref, k_ref, kx_ref, v_ref, yc_ref, *, tq, tk, fd, heads):
    qb = pl.program_id(2)
    hsl = [slice(h * fd, (h + 1) * fd) for h in range(heads)]
    xsl = [slice(h * V7X_LANES, (h + 1) * V7X_LANES) for h in range(heads)]
    qs = [jnp.concatenate([q_ref[:, hsl[h]], qx_ref[:, xsl[h]]], axis=1) for h in range(heads)]
    row = lax.broadcasted_iota(jnp.int32, (tq, tk), 0)
    col = lax.broadcasted_iota(jnp.int32, (tq, tk), 1)

    def block(kb, carry, diag_offset=None):
        ks = pl.multiple_of(kb * tk, tk)
        out = []
        for h in range(heads):
            m_i, l_i, acc = carry[h]
            k = jnp.concatenate([k_ref[pl.ds(ks, tk), hsl[h]], kx_ref[pl.ds(ks, tk), xsl[h]]], axis=1)
            s = _dot_nt(qs[h], k)
            if diag_offset is not None:
                s = jnp.where(row >= col + diag_offset, s, NEG_INF)
            m_new = jnp.maximum(m_i, jnp.max(s, axis=1, keepdims=True))
            alpha = jnp.exp2(m_i - m_new)
            p = jnp.exp2(s - m_new)
            l_new = alpha * l_i + jnp.sum(p, axis=1, keepdims=True)
            acc_new = alpha * acc + _dot(p.astype(BF16), v_ref[pl.ds(ks, tk), hsl[h]])
            out.append((m_new, l_new, acc_new))
        return tuple(out)

    init = tuple((jnp.full((tq, 1), NEG_INF, F32), jnp.zeros((tq, 1), F32), jnp.zeros((tq, fd), F32))
                 for _ in range(heads))
    per_q = tq // tk
    fin = lax.fori_loop(0, qb * per_q, block, init)
    for j in range(per_q):
        fin = block(qb * per_q + j, fin, j * tk)
    for h in range(heads):
        _, l_f, acc = fin[h]
        yc_ref[:, hsl[h]] = ((acc / l_f) * _silu(z_ref[:, hsl[h]].astype(F32))).astype(yc_ref.dtype)


def _flash(pm, kb, vb, qx, kx, dm):
    tq, fd, fh, seq = dm.tq, dm.fd, dm.fh, dm.seq
    nq = seq // tq
    hp = dm.flash_heads
    cq, cz = dm.off['c_q'] // (hp * fd), dm.off['c_z'] // (hp * fd)
    qrow = lambda c0: (lambda n, h, i: (n * nq + i, c0 + h))
    whole = lambda n, h, i: (n, h)
    return pl.pallas_call(
        functools.partial(_flash_kernel, tq=tq, tk=dm.tk, fd=fd, heads=hp),
        out_shape=jax.ShapeDtypeStruct((dm.m_rows, dm.fw), BF16),
        grid=(dm.batch, fh // hp, nq),
        in_specs=[pl.BlockSpec((tq, hp * fd), qrow(cq)), pl.BlockSpec((tq, hp * V7X_LANES), qrow(0)),
                  pl.BlockSpec((tq, hp * fd), qrow(cz)),
                  pl.BlockSpec((seq, hp * fd), whole), pl.BlockSpec((seq, hp * V7X_LANES), whole),
                  pl.BlockSpec((seq, hp * fd), whole)],
        out_specs=pl.BlockSpec((tq, hp * fd), qrow(0)),
        compiler_params=_cparams(3),
        name="fox_prompt",
    )(pm, qx, pm, kb, kx, vb)


def _decode_kernel(pt_ref, q_ref, knew_ref, vnew_ref, lfnew_ref, z_ref, *rest, pages_per_step):
    pb = pages_per_step
    k_refs, v_refs, lf_refs = rest[:pb], rest[pb:2 * pb], rest[2 * pb:3 * pb]
    o_ref, m_ref, l_ref, acc_ref, cs_ref, pad_ref = rest[3 * pb:]
    j = pl.program_id(1)
    page = k_refs[0].shape[0]

    @pl.when(j == 0)
    def _():
        m_ref[...] = jnp.full_like(m_ref, NEG_INF)
        l_ref[...] = jnp.zeros_like(l_ref)
        acc_ref[...] = jnp.zeros_like(acc_ref)
        cs_ref[...] = jnp.zeros_like(cs_ref)
        pad_ref[...] = jnp.zeros_like(pad_ref)

    q = q_ref[...]
    fh, fd = q.shape
    ones = jnp.ones((fd, V7X_LANES), BF16)
    diag = (lax.broadcasted_iota(jnp.int32, (page, fh, V7X_LANES), 0)
            == lax.broadcasted_iota(jnp.int32, (page, fh, V7X_LANES), 2))
    lf_rows = []
    for b in range(pb):
        pad_ref[b, :, 0:fh] = lf_refs[b][...]
        lf_rows.append(pad_ref[b].T[0:fh, :])
    cum_all = _cumsum_lanes(jnp.concatenate(lf_rows, axis=0))
    m_run, base = m_ref[:, 0:1], cs_ref[:, 0:1]
    scores = []
    for b in range(pb):
        cum = cum_all[b * fh:(b + 1) * fh, :] + base
        base = cum[:, page - 1:page]
        kq = _dot((k_refs[b][...] * q).reshape(page * fh, fd).astype(BF16), ones)
        scores.append(jnp.sum(jnp.where(diag, kq.reshape(page, fh, V7X_LANES), 0.0), axis=0) - cum * LOG2E)
    m_new = m_run
    for s in scores:
        m_new = jnp.maximum(m_new, jnp.max(s, axis=1, keepdims=True))
    alpha = jnp.exp2(m_run - m_new)
    l_run = alpha * l_ref[:, 0:1]
    acc = acc_ref[...] * alpha
    for b in range(pb):
        p = jnp.exp2(scores[b] - m_new)
        l_run = l_run + jnp.sum(p, axis=1, keepdims=True)
        p_rows = jnp.where(diag, p, 0.0).reshape(page * fh, V7X_LANES).astype(BF16)
        p_rep = _dot(p_rows, jnp.ones((V7X_LANES, fd), BF16)).reshape(page, fh, fd)
        acc = acc + jnp.sum(p_rep * v_refs[b][...], axis=0)
    m_run = m_new
    m_ref[...] = jnp.broadcast_to(m_run, m_ref.shape)
    l_ref[...] = jnp.broadcast_to(l_run, l_ref.shape)
    acc_ref[...] = acc
    cs_ref[...] = jnp.broadcast_to(base, cs_ref.shape)

    @pl.when(j == pl.num_programs(1) - 1)
    def _():
        m_old, l_old = m_ref[:, 0:1], l_ref[:, 0:1]
        s_new = (jnp.sum(knew_ref[...] * q, axis=1, keepdims=True)
                 - (cs_ref[:, 0:1] + lfnew_ref[:, 0:1]) * LOG2E)
        m_fin = jnp.maximum(m_old, s_new)
        a_fin = jnp.exp2(m_old - m_fin)
        p_new = jnp.exp2(s_new - m_fin)
        out = (acc_ref[...] * a_fin + p_new * vnew_ref[...]) / (a_fin * l_old + p_new)
        o_ref[...] = out * _silu(z_ref[...])


def _decode(page_table, q3, cache_k, cache_v, cache_lf, knew, vnew, lfnew, zs, dm, layer):
    n, fh, fd, page = dm.dec_batch, dm.fh, dm.fd, dm.page
    pb = dm.pages_per_step
    per_seq = lambda a: pl.BlockSpec((None,) + a.shape[1:], lambda i, j, pt: (i,) + (0,) * (a.ndim - 1))
    kv_spec = lambda b: pl.BlockSpec((None, None, page, fh, fd),
                                     lambda i, j, pt: (layer, pt[i, j * pb + b], 0, 0, 0))
    lf_spec = lambda b: pl.BlockSpec((None, None, page, fh), lambda i, j, pt: (layer, pt[i, j * pb + b], 0, 0))
    grid_spec = pltpu.PrefetchScalarGridSpec(
        num_scalar_prefetch=1,
        grid=(n, dm.n_pages // pb),
        in_specs=([per_seq(q3), per_seq(knew), per_seq(vnew), per_seq(lfnew), per_seq(zs)]
                  + [kv_spec(b) for b in range(pb)] + [kv_spec(b) for b in range(pb)]
                  + [lf_spec(b) for b in range(pb)]),
        out_specs=pl.BlockSpec((None, fh, fd), lambda i, j, pt: (i, 0, 0)),
        scratch_shapes=[pltpu.VMEM((fh, V7X_LANES), F32), pltpu.VMEM((fh, V7X_LANES), F32),
                        pltpu.VMEM((fh, fd), F32), pltpu.VMEM((fh, V7X_LANES), F32),
                        pltpu.VMEM((pb, page, V7X_LANES), F32)],
    )
    return pl.pallas_call(
        functools.partial(_decode_kernel, pages_per_step=pb),
        out_shape=jax.ShapeDtypeStruct((n, fh, fd), F32),
        grid_spec=grid_spec,
        compiler_params=_cparams(2),
        name="fox_sample",
    )(page_table, q3, knew, vnew, lfnew, zs, *([cache_k] * pb), *([cache_v] * pb), *([cache_lf] * pb))


def _merge_kernel(ya_ref, yb_ref, yc_ref, ga_ref, gb_ref, gc_ref, x_ref, wpa_ref, wpb_ref, wpc_ref,
                  wout_ref, lng_ref, lnb_ref, xo_ref, xb_ref, *, alpha):
    def branch(y_ref, w_ref, g_ref):
        return _sigmoid(g_ref[...].astype(F32)) * _dot(y_ref[...].astype(BF16), w_ref[...])

    merged = branch(ya_ref, wpa_ref, ga_ref) + branch(yb_ref, wpb_ref, gb_ref) + branch(yc_ref, wpc_ref, gc_ref)
    out = _dot(merged.astype(BF16), wout_ref[...])
    r = alpha * x_ref[...] + out
    mu = jnp.mean(r, axis=-1, keepdims=True)
    var = jnp.mean(jnp.square(r - mu), axis=-1, keepdims=True)
    xn = (r - mu) * lax.rsqrt(var + LN_EPS) * lng_ref[...] + lnb_ref[...]
    xo_ref[...] = xn
    xb_ref[...] = xn.astype(BF16)


def _merge(ya, yb, yc, pm, x, wts, dm, layer, tm, alpha):
    m, d = x.shape
    sw, mw, fw = dm.sw, dm.mw, dm.fw
    row = lambda i: (i, 0)
    lay3 = lambda i: (layer, 0, 0)
    g0 = dm.off['gate'] // d
    return pl.pallas_call(
        functools.partial(_merge_kernel, alpha=alpha),
        out_shape=(jax.ShapeDtypeStruct((m, d), F32), jax.ShapeDtypeStruct((m, d), BF16)),
        grid=(m // tm,),
        in_specs=[pl.BlockSpec((tm, sw), row), pl.BlockSpec((tm, mw), row), pl.BlockSpec((tm, fw), row),
                  pl.BlockSpec((tm, d), lambda i: (i, g0)), pl.BlockSpec((tm, d), lambda i: (i, g0 + 1)),
                  pl.BlockSpec((tm, d), lambda i: (i, g0 + 2)),
                  pl.BlockSpec((tm, d), row),
                  pl.BlockSpec((None, sw, d), lay3), pl.BlockSpec((None, mw, d), lay3),
                  pl.BlockSpec((None, fw, d), lay3), pl.BlockSpec((None, d, d), lay3),
                  pl.BlockSpec((None, 1, d), lay3), pl.BlockSpec((None, 1, d), lay3)],
        out_specs=(pl.BlockSpec((tm, d), row), pl.BlockSpec((tm, d), row)),
        compiler_params=_cparams(1),
        name="merge",
    )(ya, yb, yc, pm, pm, pm, x, *wts)


def _prep_in_weights(w_in, b_in, dm):
    scales = {'m_k': dm.md ** -0.5, 'c_q': dm.fd ** -0.5 * LOG2E}
    plan_main = tuple(dm.src[nm] + (dm.off[nm], scales.get(nm, 1.0)) for nm in dm.main_order)
    plan_kv = (dm.src['c_k'] + (0, 1.0), dm.src['c_v'] + (dm.fw, 1.0))
    small = tuple(dm.src[nm] for nm in ('m_i', 'm_f', 'c_f'))
    w_main, w_kv, w_small = _repack(w_in, dm, plan_main, plan_kv, small)
    bias = lambda plan: jnp.concatenate([b_in[:, a:b] * sc for a, b, _, sc in plan], axis=1)[:, None, :]
    b_small = jnp.zeros((w_in.shape[0], V7X_LANES), F32)
    for a, b in small:
        b_small = b_small.at[:, a % V7X_LANES:a % V7X_LANES + (b - a)].set(b_in[:, a:b])
    return w_main, bias(plan_main), w_kv, bias(plan_kv), w_small, b_small[:, None, :]


def _s5_mats(b_re, b_im, c_re, c_im, dm):
    eye = jnp.eye(dm.g, dtype=F32)
    bm = lambda b: jnp.einsum('lgpc,gh->lgchp', b, eye).reshape(dm.depth, dm.sw, dm.gp)
    cm = lambda c: jnp.einsum('lgcp,gh->lgphc', c, eye).reshape(dm.depth, dm.gp, dm.sw)
    bmat = jnp.concatenate([bm(b_re), bm(b_im)], axis=2).astype(BF16)
    cmat = jnp.concatenate([cm(c_re), -cm(c_im)], axis=1).astype(BF16)
    return bmat, cmat


def kernel(x_prompt, x_sample, cache_k, cache_v, cache_logf, page_table, state_ssm_re, state_ssm_im,
           state_mlstm_c, state_mlstm_n, state_mlstm_m, w_in, b_in, ssm_a_re, ssm_a_im, ssm_b_re,
           ssm_b_im, ssm_c_re, ssm_c_im, ssm_d, ssm_log_dt, w_glu, b_glu, mlstm_norm_g, w_pa, w_pb,
           w_pc, w_out, ln_g, ln_b):
    dm = _Dims(x_prompt, x_sample, cache_k, page_table, state_mlstm_c, w_in, ssm_a_re, ssm_b_re)
    depth, d = dm.depth, dm.d
    nb, nd, seq = dm.batch, dm.dec_batch, dm.seq
    alpha = (2 * depth) ** 0.25

    w_main, b_main, w_kv, b_kv, w_small, b_small = _prep_in_weights(w_in, b_in, dm)
    merge_w = (w_pa.astype(BF16), w_pb.astype(BF16), w_pc.astype(BF16), w_out.astype(BF16),
               ln_g[:, None, :], ln_b[:, None, :])
    wglu_b = w_glu.astype(BF16)
    bglu3, d3, normg3 = b_glu[:, None, :], ssm_d[:, None, :], mlstm_norm_g[:, None, :]
    page_table = page_table.astype(jnp.int32)
    heads = lambda a: a.reshape(nd, dm.fh, dm.fd)
    disc = jnp.swapaxes(_s5_disc(ssm_a_re, ssm_a_im, ssm_log_dt).reshape(4, depth, dm.gp), 0, 1)
    disc = jnp.concatenate([disc, jnp.zeros((depth, V7X_SUBLANES - 4, dm.gp), F32)], axis=1)
    bmat, cmat = _s5_mats(ssm_b_re, ssm_b_im, ssm_c_re, ssm_c_im, dm)

    xp = x_prompt.reshape(nb * seq, d)
    xs = x_sample.reshape(nd, d)
    xpb, xsb = xp.astype(BF16), xs.astype(BF16)
    cf = slice(dm.lane_cf, dm.lane_cf + dm.fh)
    outs = {k: [] for k in ('kp', 'vp', 'lfp', 'ks', 'vs', 'lfs', 'srp', 'sip', 'srs', 'sis',
                            'mcp', 'mnp', 'mmp', 'mcs', 'mns', 'mms')}
    for l in range(depth):

        pm = _inproj_main(xpb, w_main, b_main, l, dm.tm_in, dm.tn_main, BF16)
        k32, v32, s32, kb, vb = _inproj_kv(xpb, w_kv, b_kv, w_small, b_small, l, dm.tm_in, dm.fw)
        lf, bc, st, bct, qx, kx = _gates(s32, nb, seq, dm.chunk, dm.fh, dm.lane_cf)
        ya, sst = _s5_prompt(pm, dm, l, bmat, cmat, disc, d3, wglu_b, bglu3)
        yb, caug, mo = _mlstm_prompt(pm, bc, st, bct, normg3, dm, l)
        yc = _flash(pm, kb, vb, qx, kx, dm)
        xp, xpb = _merge(ya, yb, yc, pm, xp, merge_w, dm, l, dm.tm_merge, alpha)
        outs['kp'].append(k32)
        outs['vp'].append(v32)
        outs['lfp'].append(lf[:, cf])
        outs['srp'].append(sst[:, 0])
        outs['sip'].append(sst[:, 1])
        outs['mcp'].append(caug[..., :dm.md])
        outs['mnp'].append(caug[..., dm.md])
        outs['mmp'].append(mo[:, :dm.mh, 0])

        ps = _inproj_main(xsb, w_main, b_main, l, nd, dm.tn_main, F32)
        k32s, v32s, s32s, _, _ = _inproj_kv(xsb, w_kv, b_kv, w_small, b_small, l, nd, dm.fw)
        lfs = _logsig(s32s)
        yas, x1r, x1i = _s5_sample(ps, dm, l, state_ssm_re[l].reshape(nd, dm.gp),
                                   state_ssm_im[l].reshape(nd, dm.gp), bmat, cmat, disc, d3, wglu_b, bglu3)
        ybs, c1, n1, m1 = _mlstm_sample(ps, s32s, state_mlstm_c, state_mlstm_n, state_mlstm_m, normg3, dm, l)
        qs = ps[:, dm.off['c_q']:dm.off['c_q'] + dm.fw]
        zs = ps[:, dm.off['c_z']:dm.off['c_z'] + dm.fw]
        lfnew = jnp.broadcast_to(lfs[:, cf][:, :, None], (nd, dm.fh, V7X_LANES))
        ycs = _decode(page_table, heads(qs), cache_k, cache_v, cache_logf, heads(k32s), heads(v32s), lfnew,
                      heads(zs), dm, l).reshape(nd, dm.fw)
        xs, xsb = _merge(yas, ybs, ycs, ps, xs, merge_w, dm, l, nd, alpha)
        outs['ks'].append(k32s)
        outs['vs'].append(v32s)
        outs['lfs'].append(lfs[:, cf])
        outs['srs'].append(x1r)
        outs['sis'].append(x1i)
        outs['mcs'].append(c1)
        outs['mns'].append(n1)
        outs['mms'].append(m1)

    stk = lambda k, shape: jnp.stack(outs[k]).reshape((depth,) + shape)
    return (xp.reshape(nb, seq, d), xs.reshape(nd, 1, d),
            stk('kp', (nb, seq, dm.fh, dm.fd)), stk('vp', (nb, seq, dm.fh, dm.fd)), stk('lfp', (nb, seq, dm.fh)),
            stk('ks', (nd, 1, dm.fh, dm.fd)), stk('vs', (nd, 1, dm.fh, dm.fd)), stk('lfs', (nd, 1, dm.fh)),
            stk('srp', (nb, dm.g, dm.p)), stk('sip', (nb, dm.g, dm.p)),
            stk('srs', (nd, dm.g, dm.p)), stk('sis', (nd, dm.g, dm.p)),
            stk('mcp', (nb, dm.mh, dm.md, dm.md)), stk('mnp', (nb, dm.mh, dm.md)), stk('mmp', (nb, dm.mh)),
            stk('mcs', (nd, dm.mh, dm.md, dm.md)), stk('mns', (nd, dm.mh, dm.md)), stk('mms', (nd, dm.mh)))
```

```python
import functools
import math

import jax
import jax.numpy as jnp
from jax import lax
from jax.experimental import pallas as pl
from jax.experimental.pallas import tpu as pltpu

F32 = jnp.float32
BF16 = jnp.bfloat16

V7X_LANES = 128
V7X_SUBLANES = 8
V7X_VMEM_LIMIT_BYTES = 56 * 1024 * 1024

LN_EPS = 1e-5
NEG_INF = float("-inf")
LOG2E = math.log2(math.e)
V7X_MXU_DIM = 256


def _cparams(n_axes, flags=None):
    return pltpu.CompilerParams(dimension_semantics=("arbitrary",) * n_axes,
                                vmem_limit_bytes=V7X_VMEM_LIMIT_BYTES, flags=flags)


def _sigmoid(x):
    return 1.0 / (1.0 + jnp.exp(-x))


def _silu(x):
    return x * _sigmoid(x)


def _log_sigmoid(x):
    return jnp.minimum(x, 0.0) - jnp.log1p(jnp.exp(-jnp.abs(x)))


def _gelu_tanh(x):
    c = math.sqrt(2.0 / math.pi)
    return 0.5 * x * (1.0 + jnp.tanh(c * (x + 0.044715 * (x * x * x))))


def _dot(a, b):
    return jnp.dot(a, b, preferred_element_type=F32)


def _dot_nt(a, b):
    return lax.dot_general(a, b, (((1,), (1,)), ((), ())), preferred_element_type=F32)


def _split3(x):
    hi = x.astype(BF16)
    r1 = x - hi.astype(F32)
    mid = r1.astype(BF16)
    lo = (r1 - mid.astype(F32)).astype(BF16)
    return hi, mid, lo


def _cumsum_rows(x):
    n = x.shape[0]
    r = lax.broadcasted_iota(jnp.int32, (n, n), 0)
    c = lax.broadcasted_iota(jnp.int32, (n, n), 1)
    tri = (r >= c).astype(BF16)
    hi, mid, lo = _split3(x)
    return _dot(tri, hi) + _dot(tri, mid) + _dot(tri, lo)


def _cumsum_lanes(x):
    n = x.shape[1]
    r = lax.broadcasted_iota(jnp.int32, (n, n), 0)
    c = lax.broadcasted_iota(jnp.int32, (n, n), 1)
    tri = (r <= c).astype(BF16)
    hi, mid, lo = _split3(x)
    return _dot(hi, tri) + _dot(mid, tri) + _dot(lo, tri)


class _Dims:
    def __init__(self, x_prompt, x_sample, cache_k, page_table, state_mlstm_c, w_in, ssm_a_re, ssm_b_re):
        self.batch, self.seq, self.d = x_prompt.shape
        self.dec_batch = x_sample.shape[0]
        assert x_sample.shape[1] == 1
        self.depth = w_in.shape[0]
        self.g, self.p = ssm_a_re.shape[1], ssm_a_re.shape[2]
        self.gs = ssm_b_re.shape[3]
        self.sw = self.g * self.gs
        self.gp = self.g * self.p
        self.mh, self.md = state_mlstm_c.shape[2], state_mlstm_c.shape[3]
        self.mw = self.mh * self.md
        self.page, self.fh, self.fd = cache_k.shape[2], cache_k.shape[3], cache_k.shape[4]
        self.fw = self.fh * self.fd
        self.n_pages = page_table.shape[1]
        self.n_pool = cache_k.shape[1]
        d, sw, mw, fw, mh, fh = self.d, self.sw, self.mw, self.fw, self.mh, self.fh
        widths = (('ssm_u', sw), ('ssm_z', sw), ('m_q', mw), ('m_k', mw), ('m_v', mw), ('m_i', mh),
                  ('m_f', mh), ('m_o', mw), ('m_z', mw), ('c_q', fw), ('c_k', fw), ('c_v', fw),
                  ('c_f', fh), ('c_z', fw), ('gate', 3 * d))
        self.src = {}
        off = 0
        for nm, wd in widths:
            self.src[nm] = (off, off + wd)
            off += wd
        assert off == w_in.shape[2]
        self.main_order = ('gate', 'ssm_u', 'ssm_z', 'm_q', 'm_k', 'm_v', 'm_o', 'm_z', 'c_q', 'c_z')
        self.off = {}
        off = 0
        for nm in self.main_order:
            self.off[nm] = off
            off += self.src[nm][1] - self.src[nm][0]
        self.n_main = off
        self.n_kv = 2 * fw + V7X_LANES
        self.lane_li, self.lane_lf, self.lane_cf = (self.src[nm][0] % V7X_LANES for nm in ('m_i', 'm_f', 'c_f'))
        assert self.lane_li + mh <= self.lane_lf and self.lane_lf + mh <= self.lane_cf
        assert self.lane_cf + fh <= V7X_LANES
        assert self.md % V7X_LANES == 0 and self.fd % V7X_LANES == 0 and sw % V7X_LANES == 0
        assert fh == V7X_SUBLANES and self.page == V7X_LANES
        for nm in ('ssm_u', 'ssm_z', 'm_q', 'm_k', 'm_v', 'm_o', 'm_z'):
            assert self.off[nm] % sw == 0 and self.off[nm] % mw == 0
        assert self.off['c_q'] % self.fd == 0 and self.off['c_z'] % self.fd == 0
        self.m_rows = self.batch * self.seq
        self.tm_in = min(512, self.m_rows)
        self.tn_main = max(t for t in range(V7X_LANES, 3072 + 1, V7X_LANES) if self.n_main % t == 0)
        self.chunk = min(256, self.seq)
        self.s5_chunk = min(512, self.seq)
        self.tq = min(512, self.seq)
        self.tk = self.tq
        assert self.tq % self.tk == 0
        self.tm_merge = min(256, self.m_rows)
        self.pages_per_step = max(p for p in (8, 4, 2, 1) if self.n_pages % p == 0)
        self.flash_heads = max(hp for hp in (2, 1) if fh % hp == 0 and self.off['c_q'] % (hp * self.fd) == 0
                               and self.off['c_z'] % (hp * self.fd) == 0)
        self.rb_repack = math.gcd(512, sw, mw, fw)
        assert self.seq % self.chunk == 0 and self.seq % self.tq == 0 and self.seq % self.s5_chunk == 0
        assert self.m_rows % self.tm_in == 0 and self.m_rows % self.tm_merge == 0


def _mm_bias_kernel(x_ref, w_ref, b_ref, o_ref):
    acc = _dot(x_ref[...].astype(BF16), w_ref[...]) + b_ref[...]
    o_ref[...] = acc.astype(o_ref.dtype)


def _inproj_main(xb, w_all, b_all, layer, tm, tn, out_dtype):
    m, d = xb.shape
    n = w_all.shape[2]
    return pl.pallas_call(
        _mm_bias_kernel,
        out_shape=jax.ShapeDtypeStruct((m, n), out_dtype),
        grid=(n // tn, m // tm),
        in_specs=[pl.BlockSpec((tm, d), lambda j, i: (i, 0)),
                  pl.BlockSpec((None, d, tn), lambda j, i: (layer, 0, j)),
                  pl.BlockSpec((None, 1, tn), lambda j, i: (layer, 0, j))],
        out_specs=pl.BlockSpec((tm, tn), lambda j, i: (i, j)),
        compiler_params=_cparams(2),
        name="inproj_main",
    )(xb, w_all, b_all)


def _kv_kernel(x_ref, w_ref, b_ref, ws_ref, bs_ref, k32_ref, v32_ref, s32_ref, kb_ref, vb_ref, *, fw):
    x = x_ref[...].astype(BF16)
    acc = _dot(x, w_ref[...]) + b_ref[...]
    k = acc[:, :fw]
    v = acc[:, fw:]
    k32_ref[...] = k
    v32_ref[...] = v
    s32_ref[...] = _dot(x, ws_ref[...]) + bs_ref[...]
    kb_ref[...] = k.astype(BF16)
    vb_ref[...] = v.astype(BF16)


def _inproj_kv(xb, w_kv, b_kv, w_small, b_small, layer, tm, fw):
    m, d = xb.shape
    row = lambda i: (i, 0)
    lay3 = lambda i: (layer, 0, 0)
    return pl.pallas_call(
        functools.partial(_kv_kernel, fw=fw),
        out_shape=(jax.ShapeDtypeStruct((m, fw), F32), jax.ShapeDtypeStruct((m, fw), F32),
                   jax.ShapeDtypeStruct((m, V7X_LANES), F32),
                   jax.ShapeDtypeStruct((m, fw), BF16), jax.ShapeDtypeStruct((m, fw), BF16)),
        grid=(m // tm,),
        in_specs=[pl.BlockSpec((tm, d), row),
                  pl.BlockSpec((None, d, 2 * fw), lay3), pl.BlockSpec((None, 1, 2 * fw), lay3),
                  pl.BlockSpec((None, d, V7X_LANES), lay3), pl.BlockSpec((None, 1, V7X_LANES), lay3)],
        out_specs=(pl.BlockSpec((tm, fw), row), pl.BlockSpec((tm, fw), row),
                   pl.BlockSpec((tm, V7X_LANES), row),
                   pl.BlockSpec((tm, fw), row), pl.BlockSpec((tm, fw), row)),
        compiler_params=_cparams(1),
        name="inproj_kv",
    )(xb, w_kv, b_kv, w_small, b_small)


def _repack_kernel(tbl_ref, wt_ref, o_ref, *, scales):
    sid = tbl_ref[1, pl.program_id(1)]
    scale = jnp.where(sid == 1, scales[0], jnp.where(sid == 2, scales[1], 1.0)).astype(F32)
    o_ref[...] = (wt_ref[...] * scale).T.astype(BF16)


def _repack(wt, table, n_out, dm, scales):
    depth, _, d = wt.shape
    rb = dm.rb_repack
    grid_spec = pltpu.PrefetchScalarGridSpec(
        num_scalar_prefetch=1,
        grid=(depth, n_out // rb),
        in_specs=[pl.BlockSpec((None, pl.Element(rb), pl.Element(d)),
                               lambda l, b, tbl: (l, tbl[0, b] * V7X_SUBLANES, 0))],
        out_specs=pl.BlockSpec((None, d, rb), lambda l, b, tbl: (l, 0, b)),
    )
    return pl.pallas_call(
        functools.partial(_repack_kernel, scales=scales),
        out_shape=jax.ShapeDtypeStruct((depth, d, n_out), BF16),
        grid_spec=grid_spec,
        compiler_params=_cparams(2),
        name="repack_w_in",
    )(table, wt)


def _repack_small_kernel(*refs, rows):
    o_ref = refs[-1]
    row = lax.broadcasted_iota(jnp.int32, refs[0].shape, 0)
    small = jnp.zeros(refs[0].shape, F32)
    for w_ref, (lo, hi) in zip(refs[:-1], rows):
        small = jnp.where((row >= lo) & (row < hi), w_ref[...], small)
    o_ref[...] = small.T.astype(BF16)


def _repack_small(wt, plan_small):
    depth, _, d = wt.shape
    win = lambda a: pl.BlockSpec((None, pl.Element(V7X_LANES), pl.Element(d)),
                                 lambda l, a=a: (l, a - a % V7X_LANES, 0))
    rows = tuple((a % V7X_LANES, a % V7X_LANES + (b - a)) for a, b in plan_small)
    return pl.pallas_call(
        functools.partial(_repack_small_kernel, rows=rows),
        out_shape=jax.ShapeDtypeStruct((depth, d, V7X_LANES), BF16),
        grid=(depth,),
        in_specs=[win(a) for a, _ in plan_small],
        out_specs=pl.BlockSpec((None, d, V7X_LANES), lambda l: (l, 0, 0)),
        compiler_params=_cparams(1),
        name="repack_w_small",
    )(*([wt] * len(plan_small)))


def _gates_kernel(s_ref, lf_ref, bc_ref, st_ref, bct_ref, qx_ref, kx_ref, carry_ref, *, fh, lane_cf):
    @pl.when(pl.program_id(1) == 0)
    def _():
        carry_ref[...] = jnp.zeros_like(carry_ref)

    s = s_ref[...]
    lf = _log_sigmoid(s)
    bc = _cumsum_rows(lf)
    fc = bc + carry_ref[0:1, :]
    carry_ref[...] = jnp.broadcast_to(fc[-1:, :], carry_ref.shape)
    lf_ref[...] = lf
    bc_ref[...] = bc
    st_ref[...] = s.T
    bct_ref[...] = bc.T
    lane = lax.broadcasted_iota(jnp.int32, s.shape, 1)
    for h in range(fh):
        hi, mid, lo = (t.astype(F32) for t in _split3(fc[:, lane_cf + h:lane_cf + h + 1] * LOG2E))
        qx = jnp.where(lane == 0, hi, jnp.where(lane == 1, mid, jnp.where(lane == 2, lo,
                       jnp.where(lane < 6, 1.0, 0.0))))
        kx = jnp.where(lane < 3, 1.0, jnp.where(lane == 3, -hi, jnp.where(lane == 4, -mid,
                       jnp.where(lane == 5, -lo, 0.0))))
        hs = slice(h * V7X_LANES, (h + 1) * V7X_LANES)
        qx_ref[:, hs] = qx.astype(BF16)
        kx_ref[:, hs] = kx.astype(BF16)


def _gates(s32, n_seq, seq, chunk, fh, lane_cf):
    m = s32.shape[0]
    nc = seq // chunk
    row = pl.BlockSpec((chunk, V7X_LANES), lambda n, c: (n * nc + c, 0))
    tr = pl.BlockSpec((None, V7X_LANES, chunk), lambda n, c: (n, 0, c))
    xrow = pl.BlockSpec((chunk, fh * V7X_LANES), lambda n, c: (n * nc + c, 0))
    rs = jax.ShapeDtypeStruct((m, V7X_LANES), F32)
    ts = jax.ShapeDtypeStruct((n_seq, V7X_LANES, seq), F32)
    xs = jax.ShapeDtypeStruct((m, fh * V7X_LANES), BF16)
    return pl.pallas_call(
        functools.partial(_gates_kernel, fh=fh, lane_cf=lane_cf),
        out_shape=(rs, rs, ts, ts, xs, xs),
        grid=(n_seq, nc),
        in_specs=[row],
        out_specs=(row, row, tr, tr, xrow, xrow),
        scratch_shapes=[pltpu.VMEM((V7X_SUBLANES, V7X_LANES), F32)],
        compiler_params=_cparams(2),
        name="gates",
    )(s32)


def _logsig_kernel(s_ref, o_ref):
    o_ref[...] = _log_sigmoid(s_ref[...])


def _logsig(s32):
    return pl.pallas_call(_logsig_kernel, out_shape=jax.ShapeDtypeStruct(s32.shape, F32), name="logsig")(s32)


def _s5_disc_kernel(are_ref, aim_ref, ldt_ref, o_ref):
    a_re, a_im = are_ref[...], aim_ref[...]
    dt = jnp.exp(ldt_ref[...])
    mag = jnp.exp(a_re * dt)
    ab_re, ab_im = mag * jnp.cos(a_im * dt), mag * jnp.sin(a_im * dt)
    nr, ni = ab_re - 1.0, ab_im
    den = a_re * a_re + a_im * a_im
    o_ref[0] = ab_re
    o_ref[1] = ab_im
    o_ref[2] = (nr * a_re + ni * a_im) / den
    o_ref[3] = (ni * a_re - nr * a_im) / den


def _s5_disc(a_re, a_im, log_dt):
    ldt = jnp.broadcast_to(log_dt[..., None], a_re.shape)
    return pl.pallas_call(_s5_disc_kernel, out_shape=jax.ShapeDtypeStruct((4,) + a_re.shape, F32),
                          name="s5_disc")(a_re, a_im, ldt)


def _s5_in_map(u, bmat_ref, gp, sw):
    kt = min(V7X_MXU_DIM, sw)
    ct = gp * kt // sw
    re, im = [], []
    for i in range(sw // kt):
        uk = u[:, i * kt:(i + 1) * kt]
        re.append(_dot(uk, bmat_ref[i * kt:(i + 1) * kt, i * ct:(i + 1) * ct]))
        im.append(_dot(uk, bmat_ref[i * kt:(i + 1) * kt, gp + i * ct:gp + (i + 1) * ct]))
    return jnp.concatenate(re, axis=1), jnp.concatenate(im, axis=1)


def _s5_out_map(xr, xi, cmat_ref, gp, sw):
    nt = min(V7X_MXU_DIM, sw)
    ct = gp * nt // sw
    ys = []
    for i in range(sw // nt):
        cs, ns = slice(i * ct, (i + 1) * ct), slice(i * nt, (i + 1) * nt)
        ys.append(_dot(xr[:, cs], cmat_ref[i * ct:(i + 1) * ct, ns])
                  + _dot(xi[:, cs], cmat_ref[gp + i * ct:gp + (i + 1) * ct, ns]))
    return jnp.concatenate(ys, axis=1)


def _s5_tail(y_state, u, z, dvec, wglu, bglu):
    gact = _gelu_tanh(y_state + dvec * u)
    glu = _dot(gact.astype(BF16), wglu) + bglu
    return gact * _sigmoid(glu) * _silu(z)


def _s5_prompt_kernel(u_ref, z_ref, bmat_ref, cmat_ref, disc_ref, d_ref, wglu_ref, bglu_ref,
                      ya_ref, st_ref, buf_ref, carry_ref, *, gp, sw):
    c = pl.program_id(1)

    @pl.when(c == 0)
    def _():
        carry_ref[...] = jnp.zeros_like(carry_ref)

    u = u_ref[...]
    br, bi = _s5_in_map(u, bmat_ref, gp, sw)
    ar, ai = disc_ref[0:1, :], disc_ref[1:2, :]
    kr, ki = disc_ref[2:3, :], disc_ref[3:4, :]
    buf_ref[:, :gp] = kr * br - ki * bi
    buf_ref[:, gp:] = kr * bi + ki * br

    def step(t, carry):
        xr, xi = carry
        nr = ar * xr - ai * xi + buf_ref[pl.ds(t, 1), :gp]
        ni = ar * xi + ai * xr + buf_ref[pl.ds(t, 1), gp:]
        buf_ref[pl.ds(t, 1), :gp] = nr
        buf_ref[pl.ds(t, 1), gp:] = ni
        return nr, ni

    xr, xi = lax.fori_loop(0, u.shape[0], step, (carry_ref[0:1, :], carry_ref[1:2, :]))
    carry_ref[0:1, :] = xr
    carry_ref[1:2, :] = xi
    y_state = _s5_out_map(buf_ref[:, :gp].astype(BF16), buf_ref[:, gp:].astype(BF16), cmat_ref, gp, sw)
    ya = _s5_tail(y_state, u.astype(F32), z_ref[...].astype(F32), d_ref[...], wglu_ref[...], bglu_ref[...])
    ya_ref[...] = ya.astype(ya_ref.dtype)

    @pl.when(c == pl.num_programs(1) - 1)
    def _():
        st_ref[...] = carry_ref[...]


def _s5_prompt(pm, dm, layer, bmat, cmat, disc, dvec, wglu, bglu):
    tc = dm.s5_chunk
    nc = dm.seq // tc
    sw, gp = dm.sw, dm.gp
    cu, cz = dm.off['ssm_u'] // sw, dm.off['ssm_z'] // sw
    const2 = lambda n, c: (0, 0)
    lay3 = lambda n, c: (layer, 0, 0)
    return pl.pallas_call(
        functools.partial(_s5_prompt_kernel, gp=gp, sw=sw),
        out_shape=(jax.ShapeDtypeStruct((dm.m_rows, sw), BF16),
                   jax.ShapeDtypeStruct((dm.batch, V7X_SUBLANES, gp), F32)),
        grid=(dm.batch, nc),
        in_specs=[pl.BlockSpec((tc, sw), lambda n, c: (n * nc + c, cu)),
                  pl.BlockSpec((tc, sw), lambda n, c: (n * nc + c, cz)),
                  pl.BlockSpec((None,) + bmat.shape[1:], lay3), pl.BlockSpec((None,) + cmat.shape[1:], lay3),
                  pl.BlockSpec((None,) + disc.shape[1:], lay3),
                  pl.BlockSpec((None, 1, sw), lay3), pl.BlockSpec((None, sw, sw), lay3),
                  pl.BlockSpec((None, 1, sw), lay3)],
        out_specs=(pl.BlockSpec((tc, sw), lambda n, c: (n * nc + c, 0)),
                   pl.BlockSpec((None, V7X_SUBLANES, gp), lambda n, c: (n, 0, 0))),
        scratch_shapes=[pltpu.VMEM((tc, 2 * gp), F32), pltpu.VMEM((V7X_SUBLANES, gp), F32)],
        compiler_params=_cparams(2),
        name="s5_prompt",
    )(pm, pm, bmat, cmat, disc, dvec, wglu, bglu)


def _s5_sample_kernel(u_ref, z_ref, x0r_ref, x0i_ref, bmat_ref, cmat_ref, disc_ref, d_ref, wglu_ref,
                      bglu_ref, ya_ref, x1r_ref, x1i_ref, *, gp, sw):
    u = u_ref[...]
    br, bi = _s5_in_map(u.astype(BF16), bmat_ref, gp, sw)
    ar, ai = disc_ref[0:1, :], disc_ref[1:2, :]
    kr, ki = disc_ref[2:3, :], disc_ref[3:4, :]
    x0r, x0i = x0r_ref[...], x0i_ref[...]
    x1r = (kr * br - ki * bi) + (ar * x0r - ai * x0i)
    x1i = (kr * bi + ki * br) + (ar * x0i + ai * x0r)
    x1r_ref[...] = x1r
    x1i_ref[...] = x1i
    y_state = _s5_out_map(x1r.astype(BF16), x1i.astype(BF16), cmat_ref, gp, sw)
    ya_ref[...] = _s5_tail(y_state, u, z_ref[...], d_ref[0], wglu_ref[0], bglu_ref[0])


def _s5_sample(ps, dm, layer, x0r, x0i, bmat, cmat, disc, dvec, wglu, bglu):
    n, sw, gp = dm.dec_batch, dm.sw, dm.gp
    cu, cz = dm.off['ssm_u'] // sw, dm.off['ssm_z'] // sw
    full = lambda a: pl.BlockSpec(a.shape, lambda i: (0,) * a.ndim)
    lay3 = lambda i: (layer, 0, 0)
    return pl.pallas_call(
        functools.partial(_s5_sample_kernel, gp=gp, sw=sw),
        out_shape=(jax.ShapeDtypeStruct((n, sw), F32), jax.ShapeDtypeStruct((n, gp), F32),
                   jax.ShapeDtypeStruct((n, gp), F32)),
        grid=(1,),
        in_specs=[pl.BlockSpec((n, sw), lambda i: (0, cu)), pl.BlockSpec((n, sw), lambda i: (0, cz)),
                  full(x0r), full(x0i),
                  pl.BlockSpec((None,) + bmat.shape[1:], lay3), pl.BlockSpec((None,) + cmat.shape[1:], lay3),
                  pl.BlockSpec((None,) + disc.shape[1:], lay3),
                  pl.BlockSpec((1, 1, sw), lay3), pl.BlockSpec((1, sw, sw), lay3),
                  pl.BlockSpec((1, 1, sw), lay3)],
        out_specs=(pl.BlockSpec((n, sw), lambda i: (0, 0)), pl.BlockSpec((n, gp), lambda i: (0, 0)),
                   pl.BlockSpec((n, gp), lambda i: (0, 0))),
        compiler_params=_cparams(1),
        name="s5_sample",
    )(ps, ps, x0r, x0i, bmat, cmat, disc, dvec, wglu, bglu)


def _head_norm(h):
    mu = jnp.mean(h, axis=-1, keepdims=True)
    var = jnp.mean(jnp.square(h - mu), axis=-1, keepdims=True)
    return (h - mu) * lax.rsqrt(var + LN_EPS)


def _mlstm_prompt_kernel(q_ref, k_ref, v_ref, o_ref, z_ref, bc_ref, st_ref, bct_ref, g_ref,
                         yb_ref, cout_ref, mout_ref, cst_ref, mst_ref, *, mh, md, lane_li, lane_lf):
    c = pl.program_id(1)

    @pl.when(c == 0)
    def _():
        cst_ref[...] = jnp.zeros_like(cst_ref)
        mst_ref[...] = jnp.zeros_like(mst_ref)

    ln = q_ref.shape[0]
    row = lax.broadcasted_iota(jnp.int32, (ln, ln), 0)
    col = lax.broadcasted_iota(jnp.int32, (ln, ln), 1)
    causal = row >= col
    ones_col = (lax.broadcasted_iota(jnp.int32, (ln, V7X_LANES), 1) == 0).astype(BF16)
    for h in range(mh):
        hs = slice(h * md, (h + 1) * md)
        q, k, v = q_ref[:, hs], k_ref[:, hs], v_ref[:, hs]
        a_row = st_ref[lane_li + h:lane_li + h + 1, :] - bct_ref[lane_lf + h:lane_lf + h + 1, :]
        b_col = bc_ref[:, lane_lf + h:lane_lf + h + 1]
        m0 = mst_ref[h:h + 1, 0:1]
        amat = jnp.where(causal, a_row, NEG_INF)
        m_col = jnp.maximum(m0, jnp.max(amat, axis=1, keepdims=True))
        w = jnp.exp(amat - m_col)
        s_inter = jnp.exp(m0 - m_col)
        qk = (_dot_nt(q, k) * w).astype(BF16)
        v_aug = jnp.concatenate([v, ones_col], axis=1)
        c0 = cst_ref[h]
        num_aug = s_inter * _dot(q, c0.astype(BF16)) + _dot(qk, v_aug)
        num, den = num_aug[:, :md], num_aug[:, md:md + 1]
        hh = num / jnp.maximum(jnp.abs(den), jnp.exp(-(b_col + m_col)))
        yb = (_head_norm(hh) * g_ref[:, hs] * _sigmoid(o_ref[:, hs].astype(F32))
              * _silu(z_ref[:, hs].astype(F32)))
        yb_ref[:, hs] = yb.astype(yb_ref.dtype)
        m_end = m_col[ln - 1:ln, :]
        decay = jnp.exp(m0 - m_end)
        ws_row = jnp.exp(a_row - m_end)
        kw = (k.astype(F32).T * ws_row).astype(BF16)
        cst_ref[h] = decay * c0 + _dot(kw, v_aug)
        mst_ref[h:h + 1, :] = jnp.broadcast_to(b_col[ln - 1:ln, :] + m_end, (1, V7X_LANES))

    @pl.when(c == pl.num_programs(1) - 1)
    def _():
        cout_ref[...] = cst_ref[...]
        mout_ref[...] = mst_ref[...]


def _mlstm_prompt(pm, bc, st, bct, norm_g, dm, layer):
    ln = dm.chunk
    nc = dm.seq // ln
    mw, mh, md = dm.mw, dm.mh, dm.md
    aug = md + V7X_LANES
    rowblk = lambda nm: pl.BlockSpec((ln, mw), lambda n, c, o=dm.off[nm] // mw: (n * nc + c, o))
    tr = pl.BlockSpec((None, V7X_LANES, ln), lambda n, c: (n, 0, c))
    return pl.pallas_call(
        functools.partial(_mlstm_prompt_kernel, mh=mh, md=md, lane_li=dm.lane_li, lane_lf=dm.lane_lf),
        out_shape=(jax.ShapeDtypeStruct((dm.m_rows, mw), BF16),
                   jax.ShapeDtypeStruct((dm.batch, mh, md, aug), F32),
                   jax.ShapeDtypeStruct((dm.batch, V7X_SUBLANES, V7X_LANES), F32)),
        grid=(dm.batch, nc),
        in_specs=[rowblk('m_q'), rowblk('m_k'), rowblk('m_v'), rowblk('m_o'), rowblk('m_z'),
                  pl.BlockSpec((ln, V7X_LANES), lambda n, c: (n * nc + c, 0)), tr, tr,
                  pl.BlockSpec((None, 1, mw), lambda n, c: (layer, 0, 0))],
        out_specs=(pl.BlockSpec((ln, mw), lambda n, c: (n * nc + c, 0)),
                   pl.BlockSpec((None, mh, md, aug), lambda n, c: (n, 0, 0, 0)),
                   pl.BlockSpec((None, V7X_SUBLANES, V7X_LANES), lambda n, c: (n, 0, 0))),
        scratch_shapes=[pltpu.VMEM((mh, md, aug), F32), pltpu.VMEM((V7X_SUBLANES, V7X_LANES), F32)],
        compiler_params=_cparams(2),
        name="mlstm_prompt",
    )(pm, pm, pm, pm, pm, bc, st, bct, norm_g)


def _mlstm_sample_kernel(ps_ref, s_ref, c0_ref, n0_ref, m0_ref, g_ref, yb_ref, c1_ref, n1_ref, m1_ref,
                         *, nseq, mh, md, offs, lane_li, lane_lf):
    s = s_ref[...]
    lfa = _log_sigmoid(s)
    eye = (lax.broadcasted_iota(jnp.int32, (md, md), 0) == lax.broadcasted_iota(jnp.int32, (md, md), 1))
    for n in range(nseq):
        for h in range(mh):
            def seg(nm):
                o = offs[nm] + h * md
                return ps_ref[n:n + 1, o:o + md]
            q = seg('m_q').astype(BF16)
            k = seg('m_k').astype(BF16)
            v = seg('m_v').astype(BF16).astype(F32)
            li = s[n:n + 1, lane_li + h:lane_li + h + 1]
            lf = lfa[n:n + 1, lane_lf + h:lane_lf + h + 1]
            m0 = m0_ref[n:n + 1, h:h + 1]
            c0 = c0_ref[n, h]
            n0 = n0_ref[n, h:h + 1, :]
            inter = lf + m0
            m1 = jnp.maximum(inter, li)
            w = jnp.exp(li - m1)
            s_inter = jnp.exp(inter - m1)
            kf = k.astype(F32)
            qk = jnp.sum(q.astype(F32) * kf, axis=1, keepdims=True) * w
            qc = _dot(jnp.broadcast_to(q, (V7X_SUBLANES, md)), c0.astype(BF16))[0:1, :]
            num = s_inter * qc + qk * v
            den = s_inter * jnp.sum(q.astype(F32) * n0, axis=1, keepdims=True) + qk
            hh = num / jnp.maximum(jnp.abs(den), jnp.exp(-m1))
            go = g_ref[0, :, h * md:(h + 1) * md]
            yb = _head_norm(hh) * go * _sigmoid(seg('m_o')) * _silu(seg('m_z'))
            yb_ref[n:n + 1, h * md:(h + 1) * md] = yb
            k_col = jnp.sum(jnp.where(eye, kf, 0.0), axis=1, keepdims=True)
            c1_ref[n, h] = s_inter * c0 + (w * k_col) * v
            n1_ref[n, h:h + 1, :] = s_inter * n0 + w * kf
            m1_ref[n:n + 1, h:h + 1] = m1


def _mlstm_sample(ps, s32, c0, n0, m0, norm_g, dm, layer):
    n, mh, md, mw = dm.dec_batch, dm.mh, dm.md, dm.mw
    offs = {nm: dm.off[nm] for nm in ('m_q', 'm_k', 'm_v', 'm_o', 'm_z')}
    full = lambda a: pl.BlockSpec(a.shape, lambda i: (0,) * a.ndim)
    st4 = lambda i: (layer, 0, 0, 0, 0)
    return pl.pallas_call(
        functools.partial(_mlstm_sample_kernel, nseq=n, mh=mh, md=md, offs=offs, lane_li=dm.lane_li,
                          lane_lf=dm.lane_lf),
        out_shape=(jax.ShapeDtypeStruct((n, mw), F32), jax.ShapeDtypeStruct((n, mh, md, md), F32),
                   jax.ShapeDtypeStruct((n, mh, md), F32), jax.ShapeDtypeStruct((n, mh), F32)),
        grid=(1,),
        in_specs=[full(ps), full(s32),
                  pl.BlockSpec((None, n, mh, md, md), st4),
                  pl.BlockSpec((None, n, mh, md), lambda i: (layer, 0, 0, 0)),
                  pl.BlockSpec((None, n, mh), lambda i: (layer, 0, 0)),
                  pl.BlockSpec((1, 1, mw), lambda i: (layer, 0, 0))],
        out_specs=(pl.BlockSpec((n, mw), lambda i: (0, 0)),
                   pl.BlockSpec((n, mh, md, md), lambda i: (0, 0, 0, 0)),
                   pl.BlockSpec((n, mh, md), lambda i: (0, 0, 0)),
                   pl.BlockSpec((n, mh), lambda i: (0, 0))),
        compiler_params=_cparams(1),
        name="mlstm_sample",
    )(ps, s32, c0, n0, m0, norm_g)


def _flash_kernel(q_ref, qx_ref, z_ref, k_ref, kx_ref, v_ref, yc_ref, *, tq, tk, fd, heads):
    qb = pl.program_id(2)
    hsl = [slice(h * fd, (h + 1) * fd) for h in range(heads)]
    xsl = [slice(h * V7X_LANES, (h + 1) * V7X_LANES) for h in range(heads)]
    qs = [jnp.concatenate([q_ref[:, hsl[h]], qx_ref[:, xsl[h]]], axis=1) for h in range(heads)]
    row = lax.broadcasted_iota(jnp.int32, (tq, tk), 0)
    col = lax.broadcasted_iota(jnp.int32, (tq, tk), 1)

    def block(kb, carry, diag_offset=None):
        ks = pl.multiple_of(kb * tk, tk)
        out = []
        for h in range(heads):
            m_i, l_i, acc = carry[h]
            k = jnp.concatenate([k_ref[pl.ds(ks, tk), hsl[h]], kx_ref[pl.ds(ks, tk), xsl[h]]], axis=1)
            s = _dot_nt(qs[h], k)
            if diag_offset is not None:
                s = jnp.where(row >= col + diag_offset, s, NEG_INF)
            m_new = jnp.maximum(m_i, jnp.max(s, axis=1, keepdims=True))
            alpha = jnp.exp2(m_i - m_new)
            p = jnp.exp2(s - m_new)
            l_new = alpha * l_i + jnp.sum(p, axis=1, keepdims=True)
            acc_new = alpha * acc + _dot(p.astype(BF16), v_ref[pl.ds(ks, tk), hsl[h]])
            out.append((m_new, l_new, acc_new))
        return tuple(out)

    init = tuple((jnp.full((tq, 1), NEG_INF, F32), jnp.zeros((tq, 1), F32), jnp.zeros((tq, fd), F32))
                 for _ in range(heads))
    per_q = tq // tk
    fin = lax.fori_loop(0, qb * per_q, block, init)
    for j in range(per_q):
        fin = block(qb * per_q + j, fin, j * tk)
    for h in range(heads):
        _, l_f, acc = fin[h]
        yc_ref[:, hsl[h]] = ((acc / l_f) * _silu(z_ref[:, hsl[h]].astype(F32))).astype(yc_ref.dtype)


def _flash(pm, kb, vb, qx, kx, dm):
    tq, fd, fh, seq = dm.tq, dm.fd, dm.fh, dm.seq
    nq = seq // tq
    hp = dm.flash_heads
    cq, cz = dm.off['c_q'] // (hp * fd), dm.off['c_z'] // (hp * fd)
    qrow = lambda c0: (lambda n, h, i: (n * nq + i, c0 + h))
    whole = lambda n, h, i: (n, h)
    return pl.pallas_call(
        functools.partial(_flash_kernel, tq=tq, tk=dm.tk, fd=fd, heads=hp),
        out_shape=jax.ShapeDtypeStruct((dm.m_rows, dm.fw), BF16),
        grid=(dm.batch, fh // hp, nq),
        in_specs=[pl.BlockSpec((tq, hp * fd), qrow(cq)), pl.BlockSpec((tq, hp * V7X_LANES), qrow(0)),
                  pl.BlockSpec((tq, hp * fd), qrow(cz)),
                  pl.BlockSpec((seq, hp * fd), whole), pl.BlockSpec((seq, hp * V7X_LANES), whole),
                  pl.BlockSpec((seq, hp * fd), whole)],
        out_specs=pl.BlockSpec((tq, hp * fd), qrow(0)),
        compiler_params=_cparams(3),
        name="fox_prompt",
    )(pm, qx, pm, kb, kx, vb)


def _decode_kernel(pt_ref, q_ref, knew_ref, vnew_ref, lfnew_ref, z_ref, *rest, pages_per_step):
    pb = pages_per_step
    k_refs, v_refs, lft_refs = rest[:pb], rest[pb:2 * pb], rest[2 * pb:3 * pb]
    o_ref, m_ref, l_ref, acc_ref, cs_ref = rest[3 * pb:]
    j = pl.program_id(1)
    page = k_refs[0].shape[0]

    @pl.when(j == 0)
    def _():
        m_ref[...] = jnp.full_like(m_ref, NEG_INF)
        l_ref[...] = jnp.zeros_like(l_ref)
        acc_ref[...] = jnp.zeros_like(acc_ref)
        cs_ref[...] = jnp.zeros_like(cs_ref)

    q = q_ref[...]
    fh, fd = q.shape
    ones = jnp.ones((fd, V7X_LANES), BF16)
    diag = (lax.broadcasted_iota(jnp.int32, (page, fh, V7X_LANES), 0)
            == lax.broadcasted_iota(jnp.int32, (page, fh, V7X_LANES), 2))
    cum_all = _cumsum_lanes(jnp.concatenate([r[...] for r in lft_refs], axis=0))
    m_run, base = m_ref[:, 0:1], cs_ref[:, 0:1]
    scores = []
    for b in range(pb):
        cum = cum_all[b * fh:(b + 1) * fh, :] + base
        base = cum[:, page - 1:page]
        kq = _dot((k_refs[b][...] * q).reshape(page * fh, fd).astype(BF16), ones)
        scores.append(jnp.sum(jnp.where(diag, kq.reshape(page, fh, V7X_LANES), 0.0), axis=0) - cum * LOG2E)
    m_new = m_run
    for s in scores:
        m_new = jnp.maximum(m_new, jnp.max(s, axis=1, keepdims=True))
    alpha = jnp.exp2(m_run - m_new)
    l_run = alpha * l_ref[:, 0:1]
    acc = acc_ref[...] * alpha
    for b in range(pb):
        p = jnp.exp2(scores[b] - m_new)
        l_run = l_run + jnp.sum(p, axis=1, keepdims=True)
        p_rows = jnp.where(diag, p, 0.0).reshape(page * fh, V7X_LANES).astype(BF16)
        p_rep = _dot(p_rows, jnp.ones((V7X_LANES, fd), BF16)).reshape(page, fh, fd)
        acc = acc + jnp.sum(p_rep * v_refs[b][...], axis=0)
    m_run = m_new
    m_ref[...] = jnp.broadcast_to(m_run, m_ref.shape)
    l_ref[...] = jnp.broadcast_to(l_run, l_ref.shape)
    acc_ref[...] = acc
    cs_ref[...] = jnp.broadcast_to(base, cs_ref.shape)

    @pl.when(j == pl.num_programs(1) - 1)
    def _():
        m_old, l_old = m_ref[:, 0:1], l_ref[:, 0:1]
        s_new = (jnp.sum(knew_ref[...] * q, axis=1, keepdims=True)
                 - (cs_ref[:, 0:1] + lfnew_ref[:, 0:1]) * LOG2E)
        m_fin = jnp.maximum(m_old, s_new)
        a_fin = jnp.exp2(m_old - m_fin)
        p_new = jnp.exp2(s_new - m_fin)
        out = (acc_ref[...] * a_fin + p_new * vnew_ref[...]) / (a_fin * l_old + p_new)
        o_ref[...] = out * _silu(z_ref[...])


def _decode(page_table, q3, cache_k, cache_v, cache_lf, knew, vnew, lfnew, zs, dm, layer):
    n, fh, fd, page = dm.dec_batch, dm.fh, dm.fd, dm.page
    pb = dm.pages_per_step
    per_seq = lambda a: pl.BlockSpec((None,) + a.shape[1:], lambda i, j, pt: (i,) + (0,) * (a.ndim - 1))
    kv_spec = lambda b: pl.BlockSpec((None, None, page, fh, fd),
                                     lambda i, j, pt: (layer, pt[i, j * pb + b], 0, 0, 0))
    lf_spec = lambda b: pl.BlockSpec((None, None, fh, page), lambda i, j, pt: (layer, pt[i, j * pb + b], 0, 0))
    grid_spec = pltpu.PrefetchScalarGridSpec(
        num_scalar_prefetch=1,
        grid=(n, dm.n_pages // pb),
        in_specs=([per_seq(q3), per_seq(knew), per_seq(vnew), per_seq(lfnew), per_seq(zs)]
                  + [kv_spec(b) for b in range(pb)] + [kv_spec(b) for b in range(pb)]
                  + [lf_spec(b) for b in range(pb)]),
        out_specs=pl.BlockSpec((None, fh, fd), lambda i, j, pt: (i, 0, 0)),
        scratch_shapes=[pltpu.VMEM((fh, V7X_LANES), F32), pltpu.VMEM((fh, V7X_LANES), F32),
                        pltpu.VMEM((fh, fd), F32), pltpu.VMEM((fh, V7X_LANES), F32)],
    )
    return pl.pallas_call(
        functools.partial(_decode_kernel, pages_per_step=pb),
        out_shape=jax.ShapeDtypeStruct((n, fh, fd), F32),
        grid_spec=grid_spec,
        compiler_params=_cparams(2),
        name="fox_sample",
    )(page_table, q3, knew, vnew, lfnew, zs, *([cache_k] * pb), *([cache_v] * pb), *([cache_lf] * pb))


def _merge_kernel(ya_ref, yb_ref, yc_ref, ga_ref, gb_ref, gc_ref, x_ref, wpa_ref, wpb_ref, wpc_ref,
                  wout_ref, lng_ref, lnb_ref, xo_ref, xb_ref, *, alpha):
    def branch(y_ref, w_ref, g_ref):
        return _sigmoid(g_ref[...].astype(F32)) * _dot(y_ref[...].astype(BF16), w_ref[...])

    merged = branch(ya_ref, wpa_ref, ga_ref) + branch(yb_ref, wpb_ref, gb_ref) + branch(yc_ref, wpc_ref, gc_ref)
    out = _dot(merged.astype(BF16), wout_ref[...])
    r = alpha * x_ref[...] + out
    mu = jnp.mean(r, axis=-1, keepdims=True)
    var = jnp.mean(jnp.square(r - mu), axis=-1, keepdims=True)
    xn = (r - mu) * lax.rsqrt(var + LN_EPS) * lng_ref[...] + lnb_ref[...]
    xo_ref[...] = xn
    xb_ref[...] = xn.astype(BF16)


def _merge(ya, yb, yc, pm, x, wts, dm, layer, tm, alpha):
    m, d = x.shape
    sw, mw, fw = dm.sw, dm.mw, dm.fw
    row = lambda i: (i, 0)
    lay3 = lambda i: (layer, 0, 0)
    g0 = dm.off['gate'] // d
    return pl.pallas_call(
        functools.partial(_merge_kernel, alpha=alpha),
        out_shape=(jax.ShapeDtypeStruct((m, d), F32), jax.ShapeDtypeStruct((m, d), BF16)),
        grid=(m // tm,),
        in_specs=[pl.BlockSpec((tm, sw), row), pl.BlockSpec((tm, mw), row), pl.BlockSpec((tm, fw), row),
                  pl.BlockSpec((tm, d), lambda i: (i, g0)), pl.BlockSpec((tm, d), lambda i: (i, g0 + 1)),
                  pl.BlockSpec((tm, d), lambda i: (i, g0 + 2)),
                  pl.BlockSpec((tm, d), row),
                  pl.BlockSpec((None, sw, d), lay3), pl.BlockSpec((None, mw, d), lay3),
                  pl.BlockSpec((None, fw, d), lay3), pl.BlockSpec((None, d, d), lay3),
                  pl.BlockSpec((None, 1, d), lay3), pl.BlockSpec((None, 1, d), lay3)],
        out_specs=(pl.BlockSpec((tm, d), row), pl.BlockSpec((tm, d), row)),
        compiler_params=_cparams(1),
        name="merge",
    )(ya, yb, yc, pm, pm, pm, x, *wts)


def _prep_in_weights(w_in, b_in, dm):
    scales = {'m_k': dm.md ** -0.5, 'c_q': dm.fd ** -0.5 * LOG2E}
    plan_main = tuple(dm.src[nm] + (dm.off[nm], scales.get(nm, 1.0)) for nm in dm.main_order)
    plan_kv = (dm.src['c_k'] + (0, 1.0), dm.src['c_v'] + (dm.fw, 1.0))
    small = tuple(dm.src[nm] for nm in ('m_i', 'm_f', 'c_f'))
    wt = jnp.swapaxes(w_in, 1, 2)
    rb = dm.rb_repack
    scale_vals = tuple(sorted({sc for *_, sc in plan_main + plan_kv if sc != 1.0}))
    assert len(scale_vals) <= 2

    def table(plan):
        assert all(a % V7X_SUBLANES == 0 for a, *_ in plan)
        rows = [((a + o) // V7X_SUBLANES, 0 if sc == 1.0 else 1 + scale_vals.index(sc))
                for a, b, _, sc in plan for o in range(0, b - a, rb)]
        return jnp.array(list(zip(*rows)), jnp.int32)

    n_kv = 2 * dm.fw
    w_main = _repack(wt, table(plan_main), dm.n_main, dm, scale_vals + (1.0, 1.0))
    w_kv = _repack(wt, table(plan_kv), n_kv, dm, scale_vals + (1.0, 1.0))
    w_small = _repack_small(wt, small)
    bias = lambda plan: jnp.concatenate([b_in[:, a:b] * sc for a, b, _, sc in plan], axis=1)[:, None, :]
    b_small = jnp.zeros((w_in.shape[0], V7X_LANES), F32)
    for a, b in small:
        b_small = b_small.at[:, a % V7X_LANES:a % V7X_LANES + (b - a)].set(b_in[:, a:b])
    return w_main, bias(plan_main), w_kv, bias(plan_kv), w_small, b_small[:, None, :]


def _s5_mats(b_re, b_im, c_re, c_im, dm):
    eye = jnp.eye(dm.g, dtype=F32)
    bm = lambda b: jnp.einsum('lgpc,gh->lgchp', b, eye).reshape(dm.depth, dm.sw, dm.gp)
    cm = lambda c: jnp.einsum('lgcp,gh->lgphc', c, eye).reshape(dm.depth, dm.gp, dm.sw)
    bmat = jnp.concatenate([bm(b_re), bm(b_im)], axis=2).astype(BF16)
    cmat = jnp.concatenate([cm(c_re), -cm(c_im)], axis=1).astype(BF16)
    return bmat, cmat


def kernel(x_prompt, x_sample, cache_k, cache_v, cache_logf, page_table, state_ssm_re, state_ssm_im,
           state_mlstm_c, state_mlstm_n, state_mlstm_m, w_in, b_in, ssm_a_re, ssm_a_im, ssm_b_re,
           ssm_b_im, ssm_c_re, ssm_c_im, ssm_d, ssm_log_dt, w_glu, b_glu, mlstm_norm_g, w_pa, w_pb,
           w_pc, w_out, ln_g, ln_b):
    dm = _Dims(x_prompt, x_sample, cache_k, page_table, state_mlstm_c, w_in, ssm_a_re, ssm_b_re)
    depth, d = dm.depth, dm.d
    nb, nd, seq = dm.batch, dm.dec_batch, dm.seq
    alpha = (2 * depth) ** 0.25

    w_main, b_main, w_kv, b_kv, w_small, b_small = _prep_in_weights(w_in, b_in, dm)
    merge_w = (w_pa.astype(BF16), w_pb.astype(BF16), w_pc.astype(BF16), w_out.astype(BF16),
               ln_g[:, None, :], ln_b[:, None, :])
    wglu_b = w_glu.astype(BF16)
    bglu3, d3, normg3 = b_glu[:, None, :], ssm_d[:, None, :], mlstm_norm_g[:, None, :]
    clft = jnp.swapaxes(cache_logf, 2, 3)
    page_table = page_table.astype(jnp.int32)
    heads = lambda a: a.reshape(nd, dm.fh, dm.fd)
    disc = jnp.swapaxes(_s5_disc(ssm_a_re, ssm_a_im, ssm_log_dt).reshape(4, depth, dm.gp), 0, 1)
    disc = jnp.concatenate([disc, jnp.zeros((depth, V7X_SUBLANES - 4, dm.gp), F32)], axis=1)
    bmat, cmat = _s5_mats(ssm_b_re, ssm_b_im, ssm_c_re, ssm_c_im, dm)

    xp = x_prompt.reshape(nb * seq, d)
    xs = x_sample.reshape(nd, d)
    xpb, xsb = xp.astype(BF16), xs.astype(BF16)
    cf = slice(dm.lane_cf, dm.lane_cf + dm.fh)
    outs = {k: [] for k in ('kp', 'vp', 'lfp', 'ks', 'vs', 'lfs', 'srp', 'sip', 'srs', 'sis',
                            'mcp', 'mnp', 'mmp', 'mcs', 'mns', 'mms')}
    for l in range(depth):

        pm = _inproj_main(xpb, w_main, b_main, l, dm.tm_in, dm.tn_main, BF16)
        k32, v32, s32, kb, vb = _inproj_kv(xpb, w_kv, b_kv, w_small, b_small, l, dm.tm_in, dm.fw)
        lf, bc, st, bct, qx, kx = _gates(s32, nb, seq, dm.chunk, dm.fh, dm.lane_cf)
        ya, sst = _s5_prompt(pm, dm, l, bmat, cmat, disc, d3, wglu_b, bglu3)
        yb, caug, mo = _mlstm_prompt(pm, bc, st, bct, normg3, dm, l)
        yc = _flash(pm, kb, vb, qx, kx, dm)
        xp, xpb = _merge(ya, yb, yc, pm, xp, merge_w, dm, l, dm.tm_merge, alpha)
        outs['kp'].append(k32)
        outs['vp'].append(v32)
        outs['lfp'].append(lf[:, cf])
        outs['srp'].append(sst[:, 0])
        outs['sip'].append(sst[:, 1])
        outs['mcp'].append(caug[..., :dm.md])
        outs['mnp'].append(caug[..., dm.md])
        outs['mmp'].append(mo[:, :dm.mh, 0])

        ps = _inproj_main(xsb, w_main, b_main, l, nd, dm.tn_main, F32)
        k32s, v32s, s32s, _, _ = _inproj_kv(xsb, w_kv, b_kv, w_small, b_small, l, nd, dm.fw)
        lfs = _logsig(s32s)
        yas, x1r, x1i = _s5_sample(ps, dm, l, state_ssm_re[l].reshape(nd, dm.gp),
                                   state_ssm_im[l].reshape(nd, dm.gp), bmat, cmat, disc, d3, wglu_b, bglu3)
        ybs, c1, n1, m1 = _mlstm_sample(ps, s32s, state_mlstm_c, state_mlstm_n, state_mlstm_m, normg3, dm, l)
        qs = ps[:, dm.off['c_q']:dm.off['c_q'] + dm.fw]
        zs = ps[:, dm.off['c_z']:dm.off['c_z'] + dm.fw]
        lfnew = jnp.broadcast_to(lfs[:, cf][:, :, None], (nd, dm.fh, V7X_LANES))
        ycs = _decode(page_table, heads(qs), cache_k, cache_v, clft, heads(k32s), heads(v32s), lfnew,
                      heads(zs), dm, l).reshape(nd, dm.fw)
        xs, xsb = _merge(yas, ybs, ycs, ps, xs, merge_w, dm, l, nd, alpha)
        outs['ks'].append(k32s)
        outs['vs'].append(v32s)
        outs['lfs'].append(lfs[:, cf])
        outs['srs'].append(x1r)
        outs['sis'].append(x1i)
        outs['mcs'].append(c1)
        outs['mns'].append(n1)
        outs['mms'].append(m1)

    stk = lambda k, shape: jnp.stack(outs[k]).reshape((depth,) + shape)
    return (xp.reshape(nb, seq, d), xs.reshape(nd, 1, d),
            stk('kp', (nb, seq, dm.fh, dm.fd)), stk('vp', (nb, seq, dm.fh, dm.fd)), stk('lfp', (nb, seq, dm.fh)),
            stk('ks', (nd, 1, dm.fh, dm.fd)), stk('vs', (nd, 1, dm.fh, dm.fd)), stk('lfs', (nd, 1, dm.fh)),
            stk('srp', (nb, dm.g, dm.p)), stk('sip', (nb, dm.g, dm.p)),
            stk('srs', (nd, dm.g, dm.p)), stk('sis', (nd, dm.g, dm.p)),
            stk('mcp', (nb, dm.mh, dm.md, dm.md)), stk('mnp', (nb, dm.mh, dm.md)), stk('mmp', (nb, dm.mh)),
            stk('mcs', (nd, dm.mh, dm.md, dm.md)), stk('mns', (nd, dm.mh, dm.md)), stk('mms', (nd, dm.mh)))
```
